```python
import math
import jax, jax.numpy as jnp
from jax import lax
import numpy as np


D_MODEL = 4096
BATCH = 4
SEQ = 4096
DEPTH = 4

CTX_LEN = 256
GRID_W = 64
N_MIXERS = 4
ATTN_KINDS = (1, 2)
MOD_RANK = 256
EPS = 1e-6
FOURIER_GROUPS = 8
HEAD_DIM = 128
ROPE_THETA = 10000.0
Q_BLOCK = 128
DIFF_HEADS = D_MODEL // (2 * HEAD_DIM)
GQA_HEADS = D_MODEL // HEAD_DIM
GQA_KV_HEADS = GQA_HEADS // 4
GQA_GROUP = GQA_HEADS // GQA_KV_HEADS
GQA_QKV = (GQA_HEADS + 2 * GQA_KV_HEADS) * HEAD_DIM
CONV_WIDTH = 31
N_EXPERTS = 32
TOP_K = 4
EXPERT_DIM = 256
SWIGLU_LIMIT = 7.0
SWIGLU_ALPHA = 1.702

kernel_name = 'hybrid_latent_diffusion_block'


def rms_norm(x, g):
    xf = x.astype(jnp.float32)
    y = xf * lax.rsqrt(jnp.mean(xf * xf, axis=-1, keepdims=True) + EPS)
    return (y * g.astype(jnp.float32)).astype(x.dtype)


def layer_norm(x, g, b):
    xf = x.astype(jnp.float32)
    mu = jnp.mean(xf, axis=-1, keepdims=True)
    var = jnp.mean(jnp.square(xf - mu), axis=-1, keepdims=True)
    y = (xf - mu) * lax.rsqrt(var + EPS)
    return (y * g.astype(jnp.float32) + b.astype(jnp.float32)).astype(x.dtype)


def adaln(cond, a, b_mat, bias):
    m = (jax.nn.silu(cond) @ a) @ b_mat + bias
    return jnp.split(m, 6, axis=-1)


def modulate(h, g, shift, scale):
    return rms_norm(h, g) * (1.0 + scale) + shift


def axial_rope_table(rows):
    row = jnp.repeat(jnp.arange(rows, dtype=jnp.float32), GRID_W)
    col = jnp.tile(jnp.arange(GRID_W, dtype=jnp.float32), rows)
    nf = HEAD_DIM // 4
    inv = ROPE_THETA ** (-jnp.arange(nf, dtype=jnp.float32) / nf)
    ang = jnp.stack([row, col], axis=-1)[:, :, None] * inv
    return jnp.cos(ang), jnp.sin(ang)


def apply_rope(x, cos, sin):
    shp = x.shape
    nf = shp[-1] // 4
    xr = x.astype(jnp.float32).reshape(shp[:-1] + (2, 2, nf))
    x1, x2 = xr[..., 0, :], xr[..., 1, :]
    bshape = (1, shp[1]) + (1,) * (x.ndim - 3) + (2, nf)
    cs, sn = cos.reshape(bshape), sin.reshape(bshape)
    out = jnp.stack([x1 * cs - x2 * sn, x2 * cs + x1 * sn], axis=-2)
    return out.reshape(shp).astype(x.dtype)


def fourier_mix(u):
    b, l, d = u.shape
    ug = u.astype(jnp.float32).reshape(b, l, FOURIER_GROUPS, d // FOURIER_GROUPS)
    f = jnp.fft.fft2(ug, axes=(1, 3), norm='ortho').real
    return f.reshape(b, l, d).astype(u.dtype)


def diff_sweep(q, k, v, lam):
    b, lq, h, _, d = q.shape
    qb = jnp.moveaxis(q.reshape(b, lq // Q_BLOCK, Q_BLOCK, h, 2, d), 1, 0)

    def block(qblk):
        s = jnp.einsum('bqhmd,blhmd->bhmql', qblk, k, preferred_element_type=jnp.float32)
        p = jax.nn.softmax(s, axis=-1)
        a = (p[:, :, 0] - lam * p[:, :, 1]).astype(v.dtype)
        return jnp.einsum('bhql,blhe->bqhe', a, v)

    o = lax.map(block, qb)
    return jnp.moveaxis(o, 0, 1).reshape(b, lq, h, v.shape[-1])


def diff_attention(u_lat, u_ctx, w_in, lam_p, subln_g, lam_init, cos, sin, with_ctx_out):
    d_model = u_lat.shape[-1]

    def project(u):
        b, l, _ = u.shape
        qkv = u @ w_in
        q = qkv[..., :d_model].reshape(b, l, DIFF_HEADS, 2, HEAD_DIM)
        k = qkv[..., d_model:2 * d_model].reshape(b, l, DIFF_HEADS, 2, HEAD_DIM)
        v = qkv[..., 2 * d_model:].reshape(b, l, DIFF_HEADS, 2 * HEAD_DIM)
        return q, k, v

    scale = HEAD_DIM ** -0.5
    q_l, k_l, v_l = project(u_lat)
    q_l, k_l = apply_rope(q_l, cos, sin), apply_rope(k_l, cos, sin)
    q_c, k_c, v_c = project(u_ctx)
    lp = lam_p.astype(jnp.float32)
    lam = jnp.exp(jnp.sum(lp[0] * lp[1])) - jnp.exp(jnp.sum(lp[2] * lp[3])) + lam_init

    def post(o):
        b, l = o.shape[:2]
        return (rms_norm(o, subln_g) * (1.0 - lam_init)).reshape(b, l, d_model)

    k_all = jnp.concatenate([k_c, k_l], axis=1)
    v_all = jnp.concatenate([v_c, v_l], axis=1)
    o_lat = post(diff_sweep(q_l * scale, k_all, v_all, lam))
    o_ctx = post(diff_sweep(q_c * scale, k_c, v_c, lam)) if with_ctx_out else None
    return o_lat, o_ctx


def gqa_sweep(q, k, v):
    b, lq, hk, g, d = q.shape
    qb = jnp.moveaxis(q.reshape(b, lq // Q_BLOCK, Q_BLOCK, hk, g, d), 1, 0)

    def block(qblk):
        s = jnp.einsum('bqkgd,blkd->bkgql', qblk, k, preferred_element_type=jnp.float32)
        p = jax.nn.softmax(s, axis=-1).astype(v.dtype)
        return jnp.einsum('bkgql,blkd->bqkgd', p, v)

    o = lax.map(block, qb)
    return jnp.moveaxis(o, 0, 1).reshape(b, lq, hk * g * d)


def gqa_attention(u_lat, u_ctx, w_in, qn_g, kn_g, cos, sin, with_ctx_out):
    nq, nk = GQA_HEADS * HEAD_DIM, GQA_KV_HEADS * HEAD_DIM

    def project(u):
        b, l, _ = u.shape
        qkv = u @ w_in
        q = rms_norm(qkv[..., :nq].reshape(b, l, GQA_KV_HEADS, GQA_GROUP, HEAD_DIM), qn_g)
        k = rms_norm(qkv[..., nq:nq + nk].reshape(b, l, GQA_KV_HEADS, HEAD_DIM), kn_g)
        v = qkv[..., nq + nk:].reshape(b, l, GQA_KV_HEADS, HEAD_DIM)
        return q, k, v

    scale = HEAD_DIM ** -0.5
    q_l, k_l, v_l = project(u_lat)
    q_l, k_l = apply_rope(q_l, cos, sin), apply_rope(k_l, cos, sin)
    q_c, k_c, v_c = project(u_ctx)
    k_all = jnp.concatenate([k_c, k_l], axis=1)
    v_all = jnp.concatenate([v_c, v_l], axis=1)
    o_lat = gqa_sweep(q_l * scale, k_all, v_all)
    o_ctx = gqa_sweep(q_c * scale, k_c, v_c) if with_ctx_out else None
    return o_lat, o_ctx


def conformer_conv(u, pw1, pw1_b, dw, dw_b, ln_g, ln_b):
    d = u.shape[-1]
    h = u @ pw1 + pw1_b
    h = h[..., :d] * jax.nn.sigmoid(h[..., d:])
    pad = (CONV_WIDTH - 1) // 2
    h = lax.conv_general_dilated(h, dw[:, None, :].astype(h.dtype), window_strides=(1,),
                                 padding=[(pad, pad)], dimension_numbers=('NWC', 'WIO', 'NWC'),
                                 feature_group_count=d) + dw_b
    return jax.nn.silu(layer_norm(h, ln_g, ln_b))


def moe_ffn(u, w_r, b_r, w_gu, b_gu, w_dn, b_dn):
    logits = (u @ w_r + b_r).astype(jnp.float32)
    top_v, top_i = lax.top_k(logits, TOP_K)
    top_w = jax.nn.softmax(top_v, axis=-1)
    gates = jnp.einsum('tk,tke->te', top_w,
                       jax.nn.one_hot(top_i, N_EXPERTS, dtype=jnp.float32)).astype(u.dtype)
    out = jnp.zeros_like(u)
    for e in range(N_EXPERTS):
        hgu = u @ w_gu[e] + b_gu[e]
        g_lin = jnp.minimum(hgu[:, :EXPERT_DIM], SWIGLU_LIMIT)
        up = jnp.clip(hgu[:, EXPERT_DIM:], -SWIGLU_LIMIT, SWIGLU_LIMIT)
        act = g_lin * jax.nn.sigmoid(SWIGLU_ALPHA * g_lin) * (up + 1.0)
        out = out + gates[:, e:e + 1] * (act @ w_dn[e] + b_dn[e])
    return out


def setup_inputs(seed: int = 0) -> dict:
    key = jax.random.key(seed)
    ks = iter(jax.random.split(key, 32))
    D = D_MODEL

    def nrm(shape, scale):
        return jax.random.normal(next(ks), shape, jnp.float32) * scale

    def gain(shape):
        return 1.0 + nrm(shape, 0.02)

    n_b = len(range(1, DEPTH, N_MIXERS))
    n_c = len(range(2, DEPTH, N_MIXERS))
    n_d = len(range(3, DEPTH, N_MIXERS))
    return {
        'x': nrm((BATCH, SEQ, D), 1.0),
        'c': nrm((BATCH, D), 1.0),
        'ctx': nrm((BATCH, CTX_LEN, D), 1.0),
        'c_ctx': nrm((D,), 1.0),
        'norm1_g': gain((DEPTH, D)),
        'norm2_g': gain((DEPTH, D)),
        'final_g': gain((D,)),
        'mod_a': nrm((DEPTH, D, MOD_RANK), D ** -0.5),
        'mod_b': nrm((DEPTH, MOD_RANK, 6 * D), 0.5 * MOD_RANK ** -0.5),
        'mod_bias': nrm((DEPTH, 6 * D), 0.02),
        'w_o': nrm((DEPTH, D, D), D ** -0.5),
        'b_o': nrm((DEPTH, D), 0.02),
        'diff_w_in': nrm((n_b, D, 3 * D), D ** -0.5),
        'diff_lambda': nrm((n_b, 4, HEAD_DIM), 0.1),
        'diff_subln_g': gain((n_b, 2 * HEAD_DIM)),
        'gqa_w_in': nrm((n_c, D, GQA_QKV), D ** -0.5),
        'gqa_q_norm': gain((n_c, HEAD_DIM)),
        'gqa_k_norm': gain((n_c, HEAD_DIM)),
        'conv_pw1': nrm((n_d, D, 2 * D), D ** -0.5),
        'conv_pw1_b': nrm((n_d, 2 * D), 0.02),
        'conv_dw': nrm((n_d, CONV_WIDTH, D), CONV_WIDTH ** -0.5),
        'conv_dw_b': nrm((n_d, D), 0.02),
        'conv_ln_g': gain((n_d, D)),
        'conv_ln_b': nrm((n_d, D), 0.02),
        'router_w': nrm((DEPTH, D, N_EXPERTS), D ** -0.5),
        'router_b': nrm((DEPTH, N_EXPERTS), 0.01),
        'exp_w_gu': nrm((DEPTH, N_EXPERTS, D, 2 * EXPERT_DIM), D ** -0.5),
        'exp_b_gu': nrm((DEPTH, N_EXPERTS, 2 * EXPERT_DIM), 0.02),
        'exp_w_dn': nrm((DEPTH, N_EXPERTS, EXPERT_DIM, D), EXPERT_DIM ** -0.5),
        'exp_b_dn': nrm((DEPTH, N_EXPERTS, D), 0.02),
    }


def reference(x, c, ctx, c_ctx, norm1_g, norm2_g, final_g, mod_a, mod_b, mod_bias,
              w_o, b_o, diff_w_in, diff_lambda, diff_subln_g, gqa_w_in, gqa_q_norm, gqa_k_norm,
              conv_pw1, conv_pw1_b, conv_dw, conv_dw_b, conv_ln_g, conv_ln_b,
              router_w, router_b, exp_w_gu, exp_b_gu, exp_w_dn, exp_b_dn):
    b, n_lat, d_model = x.shape
    n_ctx = ctx.shape[1]
    rows = n_lat // GRID_W
    cos, sin = axial_rope_table(rows)
    ctx_last = max([i for i in range(DEPTH) if i % N_MIXERS in ATTN_KINDS], default=-1)

    h_lat, h_ctx = x, ctx
    for i in range(DEPTH):
        kind = i % N_MIXERS
        j = i // N_MIXERS
        upd_ctx = i < ctx_last
        need_ctx = upd_ctx or kind in ATTN_KINDS

        m_lat = adaln(c[:, None, :], mod_a[i], mod_b[i], mod_bias[i])
        u_lat = modulate(h_lat, norm1_g[i], m_lat[0], m_lat[1])
        m_ctx, u_ctx = None, None
        if need_ctx:
            m_ctx = adaln(c_ctx, mod_a[i], mod_b[i], mod_bias[i])
            u_ctx = modulate(h_ctx, norm1_g[i], m_ctx[0], m_ctx[1])

        if kind == 0:
            o_lat = fourier_mix(u_lat)
            o_ctx = fourier_mix(u_ctx) if upd_ctx else None
        elif kind == 1:
            lam_init = 0.8 - 0.6 * math.exp(-0.3 * i)
            o_lat, o_ctx = diff_attention(u_lat, u_ctx, diff_w_in[j], diff_lambda[j],
                                          diff_subln_g[j], lam_init, cos, sin, upd_ctx)
        elif kind == 2:
            o_lat, o_ctx = gqa_attention(u_lat, u_ctx, gqa_w_in[j], gqa_q_norm[j],
                                         gqa_k_norm[j], cos, sin, upd_ctx)
        else:
            conv_args = (conv_pw1[j], conv_pw1_b[j], conv_dw[j], conv_dw_b[j],
                         conv_ln_g[j], conv_ln_b[j])
            o_lat = conformer_conv(u_lat, *conv_args)
            o_ctx = conformer_conv(u_ctx, *conv_args) if upd_ctx else None

        h_lat = h_lat + m_lat[2] * (o_lat @ w_o[i] + b_o[i])
        moe_args = (router_w[i], router_b[i], exp_w_gu[i], exp_b_gu[i], exp_w_dn[i], exp_b_dn[i])
        if upd_ctx:
            h_ctx = h_ctx + m_ctx[2] * (o_ctx @ w_o[i] + b_o[i])
            v_ctx = modulate(h_ctx, norm2_g[i], m_ctx[3], m_ctx[4])
            v_lat = modulate(h_lat, norm2_g[i], m_lat[3], m_lat[4])
            tok = jnp.concatenate([v_ctx, v_lat], axis=1)
            f = moe_ffn(tok.reshape(-1, d_model), *moe_args).reshape(tok.shape)
            h_ctx = h_ctx + m_ctx[5] * f[:, :n_ctx]
            h_lat = h_lat + m_lat[5] * f[:, n_ctx:]
        else:
            v_lat = modulate(h_lat, norm2_g[i], m_lat[3], m_lat[4])
            f = moe_ffn(v_lat.reshape(-1, d_model), *moe_args).reshape(v_lat.shape)
            h_lat = h_lat + m_lat[5] * f

    return rms_norm(h_lat, final_g)
```

```python
import functools
import math

import jax
import jax.numpy as jnp
from jax import lax
from jax.experimental import pallas as pl
from jax.experimental.pallas import tpu as pltpu

F32 = jnp.float32
BF16 = jnp.bfloat16

EPS = 1e-6
HEAD_DIM = 128
ROPE_THETA = 10000.0
GRID_W = 64
FOURIER_GROUPS = 8
GQA_GROUP = 4
TOP_K = 4
SWIGLU_LIMIT = 7.0
SWIGLU_ALPHA = 1.702
N_MIXERS = 4
ATTN_KINDS = (1, 2)

V7X_VMEM_BYTES = 64 * 1024 * 1024
VMEM_LIMIT = V7X_VMEM_BYTES - 8 * 1024 * 1024
LANE = 128
SUBLANE = 8


def _params(*sem):
    return pltpu.CompilerParams(dimension_semantics=sem if sem else None,
                                vmem_limit_bytes=VMEM_LIMIT)


def _tile(n, pref):
    t = min(n, pref)
    assert n % t == 0, (n, pref)
    return t


def _adaln_kernel(c_ref, a_ref, b_ref, bias_ref, o_ref):
    c = c_ref[...]
    s = c * jax.nn.sigmoid(c)
    t = jnp.dot(s, a_ref[0], preferred_element_type=F32)
    o_ref[0] = jnp.dot(t, b_ref[0], preferred_element_type=F32) + bias_ref[0]


def _adaln(cond, mod_a, mod_b, mod_bias):
    depth, d, rank = mod_a.shape
    n = mod_b.shape[-1]
    r = cond.shape[0]
    tn = _tile(n, 2048)
    return pl.pallas_call(
        _adaln_kernel,
        grid=(depth, n // tn),
        in_specs=[pl.BlockSpec((r, d), lambda i, j: (0, 0)),
                  pl.BlockSpec((1, d, rank), lambda i, j: (i, 0, 0)),
                  pl.BlockSpec((1, rank, tn), lambda i, j: (i, 0, j)),
                  pl.BlockSpec((1, 1, tn), lambda i, j: (i, 0, j))],
        out_specs=pl.BlockSpec((1, r, tn), lambda i, j: (i, 0, j)),
        out_shape=jax.ShapeDtypeStruct((depth, r, n), F32),
        compiler_params=_params(),
        name="adaln",
    )(cond, mod_a, mod_b, mod_bias.reshape(depth, 1, n))


def _mod_spec(chunk, width, row_fn, col_fn):
    return pl.BlockSpec((None, None, 1, width),
                        lambda *idx: (row_fn(*idx), chunk, 0, col_fn(*idx)))


def _rms(x):
    return x * lax.rsqrt(jnp.mean(x * x, axis=-1, keepdims=True) + EPS)


def _norm_mod_kernel(h_ref, g_ref, sh_ref, sc_ref, o_ref):
    y = _rms(h_ref[0]) * g_ref[...]
    o_ref[0] = (y * (1.0 + sc_ref[...]) + sh_ref[...]).astype(o_ref.dtype)


def _norm_mod(h, g, ml, chunk0, row_of_batch):
    b, l, d = h.shape
    tr = _tile(l, 256)
    return pl.pallas_call(
        _norm_mod_kernel,
        grid=(b, l // tr),
        in_specs=[pl.BlockSpec((1, tr, d), lambda i, j: (i, j, 0)),
                  pl.BlockSpec((1, d), lambda i, j: (0, 0)),
                  _mod_spec(chunk0, d, lambda i, j: row_of_batch(i), lambda i, j: 0),
                  _mod_spec(chunk0 + 1, d, lambda i, j: row_of_batch(i), lambda i, j: 0)],
        out_specs=pl.BlockSpec((1, tr, d), lambda i, j: (i, j, 0)),
        out_shape=jax.ShapeDtypeStruct((b, l, d), BF16),
        compiler_params=_params(),
        name="norm_mod",
    )(h, g.reshape(1, d), ml, ml)


def _final_norm_kernel(h_ref, g_ref, o_ref):
    o_ref[0] = _rms(h_ref[0]) * g_ref[...]


def _final_norm(h, g):
    b, l, d = h.shape
    tr = _tile(l, 256)
    return pl.pallas_call(
        _final_norm_kernel,
        grid=(b, l // tr),
        in_specs=[pl.BlockSpec((1, tr, d), lambda i, j: (i, j, 0)),
                  pl.BlockSpec((1, d), lambda i, j: (0, 0))],
        out_specs=pl.BlockSpec((1, tr, d), lambda i, j: (i, j, 0)),
        out_shape=jax.ShapeDtypeStruct((b, l, d), F32),
        compiler_params=_params(),
        name="final_norm",
    )(h, g.reshape(1, d))


def _linear_kernel(a_ref, w_ref, o_ref):
    o_ref[0] = jnp.dot(a_ref[...], w_ref[0], preferred_element_type=F32).astype(o_ref.dtype)


def _linear(a, w, out_dtype, tm=512, tn=512):
    m, k = a.shape
    g, _, n = w.shape
    tm, tn = _tile(m, tm), _tile(n, tn)
    return pl.pallas_call(
        _linear_kernel,
        grid=(g, m // tm, n // tn),
        in_specs=[pl.BlockSpec((tm, k), lambda b, i, j: (i, 0)),
                  pl.BlockSpec((1, k, tn), lambda b, i, j: (b, 0, j))],
        out_specs=pl.BlockSpec((1, tm, tn), lambda b, i, j: (b, i, j)),
        out_shape=jax.ShapeDtypeStruct((g, m, n), out_dtype),
        compiler_params=_params(),
        name="linear",
    )(a, w)


def _linear_residual_kernel(a_ref, w_ref, b_ref, gate_ref, h_ref, o_ref):
    acc = jnp.dot(a_ref[0], w_ref[...], preferred_element_type=F32) + b_ref[...]
    o_ref[0] = h_ref[0] + gate_ref[...] * acc


def _linear_residual(o, w, bias, h, ml, gate_chunk, row_of_batch):
    b, l, k = o.shape
    n = w.shape[1]
    tm, tn = _tile(l, 1024), _tile(n, 512)
    return pl.pallas_call(
        _linear_residual_kernel,
        grid=(b, l // tm, n // tn),
        in_specs=[pl.BlockSpec((1, tm, k), lambda bi, i, j: (bi, i, 0)),
                  pl.BlockSpec((k, tn), lambda bi, i, j: (0, j)),
                  pl.BlockSpec((1, tn), lambda bi, i, j: (0, j)),
                  _mod_spec(gate_chunk, tn, lambda bi, i, j: row_of_batch(bi), lambda bi, i, j: j),
                  pl.BlockSpec((1, tm, tn), lambda bi, i, j: (bi, i, j))],
        out_specs=pl.BlockSpec((1, tm, tn), lambda bi, i, j: (bi, i, j)),
        out_shape=jax.ShapeDtypeStruct((b, l, n), F32),
        input_output_aliases={4: 0},
        compiler_params=_params(),
        name="linear_residual",
    )(o, w, bias.reshape(1, n), ml, h)


def _linear_glu_kernel(a_ref, w1_ref, w2_ref, b1_ref, b2_ref, o_ref):
    a = a_ref[...]
    lin = jnp.dot(a, w1_ref[...], preferred_element_type=F32) + b1_ref[...]
    gate = jnp.dot(a, w2_ref[...], preferred_element_type=F32) + b2_ref[...]
    o_ref[...] = (lin * jax.nn.sigmoid(gate)).astype(o_ref.dtype)


def _linear_glu(a, w, bias):
    m, k = a.shape
    n = w.shape[1] // 2
    tm, tn = _tile(m, 1024), _tile(n, 256)
    nj = n // tn
    bias = bias.reshape(1, 2 * n)
    return pl.pallas_call(
        _linear_glu_kernel,
        grid=(m // tm, nj),
        in_specs=[pl.BlockSpec((tm, k), lambda i, j: (i, 0)),
                  pl.BlockSpec((k, tn), lambda i, j: (0, j)),
                  pl.BlockSpec((k, tn), lambda i, j: (0, j + nj)),
                  pl.BlockSpec((1, tn), lambda i, j: (0, j)),
                  pl.BlockSpec((1, tn), lambda i, j: (0, j + nj))],
        out_specs=pl.BlockSpec((tm, tn), lambda i, j: (i, j)),
        out_shape=jax.ShapeDtypeStruct((m, n), BF16),
        compiler_params=_params(),
        name="linear_glu",
    )(a, w, w, bias, bias)


def _rope_tables(rows):
    nf = HEAD_DIM // 4
    row = jnp.repeat(jnp.arange(rows, dtype=F32), GRID_W)
    col = jnp.tile(jnp.arange(GRID_W, dtype=F32), rows)
    inv = ROPE_THETA ** (-jnp.arange(nf, dtype=F32) / nf)
    ar, ac = row[:, None] * inv, col[:, None] * inv
    cos = jnp.concatenate([jnp.cos(ar), jnp.cos(ar), jnp.cos(ac), jnp.cos(ac)], axis=1)
    sin = jnp.concatenate([-jnp.sin(ar), jnp.sin(ar), -jnp.sin(ac), jnp.sin(ac)], axis=1)
    return cos, sin


def _rope(x, cos, sin):
    nf = HEAD_DIM // 4
    lane = lax.broadcasted_iota(jnp.int32, x.shape, 1)
    first = (lane % (2 * nf)) < nf
    partner = jnp.where(first, pltpu.roll(x, HEAD_DIM - nf, 1), pltpu.roll(x, nf, 1))
    return x * cos + partner * sin


def _qkv_diff_kernel(a_ref, w_ref, cos_ref, sin_ref, o_ref, *, n_q, n_k, rope, scale):
    j = pl.program_id(1)
    acc = jnp.dot(a_ref[...], w_ref[...], preferred_element_type=F32)
    tn = acc.shape[1]

    def store(mult):
        for c in range(tn // HEAD_DIM):
            x = acc[:, c * HEAD_DIM:(c + 1) * HEAD_DIM]
            if rope:
                x = _rope(x, cos_ref[...], sin_ref[...])
            o_ref[:, c * HEAD_DIM:(c + 1) * HEAD_DIM] = (x * mult).astype(o_ref.dtype)

    @pl.when(j < n_q)
    def _():
        store(scale)

    @pl.when(jnp.logical_and(j >= n_q, j < n_q + n_k))
    def _():
        store(1.0)

    @pl.when(j >= n_q + n_k)
    def _():
        o_ref[...] = acc.astype(o_ref.dtype)


def _qkv_gqa_kernel(a_ref, w_ref, cos_ref, sin_ref, qg_ref, kg_ref, o_ref, *, n_q, n_k, rope, scale):
    j = pl.program_id(1)
    acc = jnp.dot(a_ref[...], w_ref[...], preferred_element_type=F32)
    tn = acc.shape[1]

    def store(g_ref, mult):
        for c in range(tn // HEAD_DIM):
            x = _rms(acc[:, c * HEAD_DIM:(c + 1) * HEAD_DIM]) * g_ref[...]
            if rope:
                x = _rope(x, cos_ref[...], sin_ref[...])
            o_ref[:, c * HEAD_DIM:(c + 1) * HEAD_DIM] = (x * mult).astype(o_ref.dtype)

    @pl.when(j < n_q)
    def _():
        store(qg_ref, scale)

    @pl.when(jnp.logical_and(j >= n_q, j < n_q + n_k))
    def _():
        store(kg_ref, 1.0)

    @pl.when(j >= n_q + n_k)
    def _():
        o_ref[...] = acc.astype(o_ref.dtype)


def _qkv(a, w, cos, sin, n_q_cols, n_k_cols, rope, gains=None):
    m, k = a.shape
    n = w.shape[1]
    s = cos.shape[0]
    tm, tn = _tile(min(m, s), 1024), _tile(n, 512)
    assert n_q_cols % tn == 0 and n_k_cols % tn == 0
    npos = s // tm
    kw = dict(n_q=n_q_cols // tn, n_k=n_k_cols // tn, rope=rope, scale=HEAD_DIM ** -0.5)
    in_specs = [pl.BlockSpec((tm, k), lambda i, j: (i, 0)),
                pl.BlockSpec((k, tn), lambda i, j: (0, j)),
                pl.BlockSpec((tm, HEAD_DIM), lambda i, j: (i % npos, 0)),
                pl.BlockSpec((tm, HEAD_DIM), lambda i, j: (i % npos, 0))]
    args = [a, w, cos, sin]
    if gains is None:
        kernel = functools.partial(_qkv_diff_kernel, **kw)
    else:
        kernel = functools.partial(_qkv_gqa_kernel, **kw)
        in_specs += [pl.BlockSpec((1, HEAD_DIM), lambda i, j: (0, 0))] * 2
        args += [gains[0].reshape(1, HEAD_DIM), gains[1].reshape(1, HEAD_DIM)]
    return pl.pallas_call(
        kernel,
        grid=(m // tm, n // tn),
        in_specs=in_specs,
        out_specs=pl.BlockSpec((tm, tn), lambda i, j: (i, j)),
        out_shape=jax.ShapeDtypeStruct((m, n), BF16),
        compiler_params=_params(),
        name="qkv",
    )(*args)


def _dft_cos_sin(n):
    k = jnp.arange(n, dtype=jnp.int32)
    ang = ((k[:, None] * k[None, :]) % n).astype(F32) * (2.0 * math.pi / n)
    s = 1.0 / math.sqrt(n)
    return jnp.cos(ang) * s, jnp.sin(ang) * s


def _fourier_channels_kernel(u_ref, w_ref, o_ref, *, groups):
    cg = w_ref.shape[0]
    for g in range(groups):
        x = u_ref[0, :, g * cg:(g + 1) * cg]
        y = jnp.dot(x, w_ref[...], preferred_element_type=F32)
        o_ref[0, 0, :, g * cg:(g + 1) * cg] = y[:, :cg].astype(o_ref.dtype)
        o_ref[0, 1, :, g * cg:(g + 1) * cg] = y[:, cg:].astype(o_ref.dtype)


def _fourier_mix(u):
    b, l, d = u.shape
    cg = d // FOURIER_GROUPS
    cc, sc = _dft_cos_sin(cg)
    wch = jnp.concatenate([cc, -sc], axis=1).astype(BF16)
    tm = _tile(l, 512)
    y = pl.pallas_call(
        functools.partial(_fourier_channels_kernel, groups=FOURIER_GROUPS),
        grid=(b, l // tm),
        in_specs=[pl.BlockSpec((1, tm, d), lambda i, j: (i, j, 0)),
                  pl.BlockSpec((cg, 2 * cg), lambda i, j: (0, 0))],
        out_specs=pl.BlockSpec((1, 2, tm, d), lambda i, j: (i, 0, j, 0)),
        out_shape=jax.ShapeDtypeStruct((b, 2, l, d), BF16),
        compiler_params=_params(),
        name="fourier_channels",
    )(u, wch)
    cl, sl = _dft_cos_sin(l)
    wpos = jnp.concatenate([cl, sl], axis=1).astype(BF16)
    return _linear(wpos, y.reshape(b, 2 * l, d), BF16)


def _softmax_parts(s):
    m = jnp.max(s, axis=-1, keepdims=True)
    p = jnp.exp(s - m)
    return p, jnp.sum(p, axis=-1, keepdims=True)


def _nt_dot(a, b):
    return lax.dot_general(a, b, (((1,), (1,)), ((), ())), preferred_element_type=F32)


def _attn_diff_kernel(*refs, n_src, tq, lam_init):
    lam_ref, g_ref, q_ref = refs[:3]
    kv_refs = refs[3:3 + 2 * n_src]
    o_ref, kbuf, vbuf = refs[3 + 2 * n_src:]
    off = 0
    for s in range(n_src):
        k_ref, v_ref = kv_refs[2 * s], kv_refs[2 * s + 1]
        n = k_ref.shape[0]
        kbuf[off:off + n, :] = k_ref[...]
        vbuf[off:off + n, :] = v_ref[...]
        off += n
    lp = lam_ref[...]
    lam = (jnp.exp(jnp.sum(lp[0:1] * lp[1:2], axis=-1, keepdims=True))
           - jnp.exp(jnp.sum(lp[2:3] * lp[3:4], axis=-1, keepdims=True)) + lam_init)

    def body(qi, carry):
        r0 = pl.multiple_of(qi * tq, tq)
        q = q_ref[pl.ds(r0, tq), :]
        p0, l0 = _softmax_parts(_nt_dot(q[:, :HEAD_DIM], kbuf[:, :HEAD_DIM]))
        p1, l1 = _softmax_parts(_nt_dot(q[:, HEAD_DIM:], kbuf[:, HEAD_DIM:]))
        a = p0 * (1.0 / l0) - p1 * (lam / l1)
        o = jnp.dot(a.astype(BF16), vbuf[...], preferred_element_type=F32)
        o = _rms(o) * g_ref[...] * (1.0 - lam_init)
        o_ref[pl.ds(r0, tq), :] = o.astype(o_ref.dtype)
        return carry

    lax.fori_loop(0, q_ref.shape[0] // tq, body, 0)


def _attn_diff(q_src, kv_srcs, lam_p, subln_g, lam_init, d_model):
    b, lq, _ = q_src.shape
    hw = 2 * HEAD_DIM
    nh = d_model // hw
    lk = sum(x.shape[1] for x in kv_srcs)
    tq = _tile(lq, 256)
    in_specs = [pl.BlockSpec((4, HEAD_DIM), lambda bi, h: (0, 0)),
                pl.BlockSpec((1, hw), lambda bi, h: (0, 0)),
                pl.BlockSpec((None, lq, hw), lambda bi, h: (bi, 0, h))]
    args = [lam_p, subln_g.reshape(1, hw), q_src]
    for x in kv_srcs:
        in_specs += [pl.BlockSpec((None, x.shape[1], hw), lambda bi, h: (bi, 0, nh + h)),
                     pl.BlockSpec((None, x.shape[1], hw), lambda bi, h: (bi, 0, 2 * nh + h))]
        args += [x, x]
    return pl.pallas_call(
        functools.partial(_attn_diff_kernel, n_src=len(kv_srcs), tq=tq, lam_init=lam_init),
        grid=(b, nh),
        in_specs=in_specs,
        out_specs=pl.BlockSpec((None, lq, hw), lambda bi, h: (bi, 0, h)),
        out_shape=jax.ShapeDtypeStruct((b, lq, d_model), BF16),
        scratch_shapes=[pltpu.VMEM((lk, hw), BF16), pltpu.VMEM((lk, hw), BF16)],
        compiler_params=_params(),
        name="attn_diff",
    )(*args)


def _attn_gqa_kernel(*refs, n_src, tq, group):
    q_ref = refs[0]
    kv_refs = refs[1:1 + 2 * n_src]
    o_ref, kbuf, vbuf = refs[1 + 2 * n_src:]
    off = 0
    for s in range(n_src):
        k_ref, v_ref = kv_refs[2 * s], kv_refs[2 * s + 1]
        n = k_ref.shape[0]
        kbuf[off:off + n, :] = k_ref[...]
        vbuf[off:off + n, :] = v_ref[...]
        off += n

    def body(qi, carry):
        r0 = pl.multiple_of(qi * tq, tq)
        for hh in range(group):
            q = q_ref[pl.ds(r0, tq), hh * HEAD_DIM:(hh + 1) * HEAD_DIM]
            p, l = _softmax_parts(_nt_dot(q, kbuf[...]))
            o = jnp.dot(p.astype(BF16), vbuf[...], preferred_element_type=F32) * (1.0 / l)
            o_ref[pl.ds(r0, tq), hh * HEAD_DIM:(hh + 1) * HEAD_DIM] = o.astype(o_ref.dtype)
        return carry

    lax.fori_loop(0, q_ref.shape[0] // tq, body, 0)


def _attn_gqa(q_src, kv_srcs, n_heads):
    b, lq, _ = q_src.shape
    n_kv = n_heads // GQA_GROUP
    gw = GQA_GROUP * HEAD_DIM
    lk = sum(x.shape[1] for x in kv_srcs)
    tq = _tile(lq, 256)
    in_specs = [pl.BlockSpec((None, lq, gw), lambda bi, g: (bi, 0, g))]
    args = [q_src]
    for x in kv_srcs:
        in_specs += [pl.BlockSpec((None, x.shape[1], HEAD_DIM), lambda bi, g: (bi, 0, n_heads + g)),
                     pl.BlockSpec((None, x.shape[1], HEAD_DIM), lambda bi, g: (bi, 0, n_heads + n_kv + g))]
        args += [x, x]
    return pl.pallas_call(
        functools.partial(_attn_gqa_kernel, n_src=len(kv_srcs), tq=tq, group=GQA_GROUP),
        grid=(b, n_kv),
        in_specs=in_specs,
        out_specs=pl.BlockSpec((None, lq, gw), lambda bi, g: (bi, 0, g)),
        out_shape=jax.ShapeDtypeStruct((b, lq, n_heads * HEAD_DIM), BF16),
        scratch_shapes=[pltpu.VMEM((lk, HEAD_DIM), BF16), pltpu.VMEM((lk, HEAD_DIM), BF16)],
        compiler_params=_params(),
        name="attn_gqa",
    )(*args)


CONV_HALO = 16
CONV_ROWS = 8
CONV_LANES = 512


def _conv_kernel(prev_ref, cur_ref, next_ref, dw_ref, dwb_ref, g_ref, b_ref, o_ref, ext, conv, *, width):
    i = pl.program_id(1)
    last = pl.num_programs(1) - 1
    ts = cur_ref.shape[0]
    pad = (width - 1) // 2
    ext[0:CONV_HALO, :] = jnp.where(i > 0, prev_ref[...].astype(F32), 0.0)
    ext[CONV_HALO:CONV_HALO + ts, :] = cur_ref[...].astype(F32)
    ext[CONV_HALO + ts:, :] = jnp.where(i < last, next_ref[...].astype(F32), 0.0)

    def body(r, carry):
        r0 = pl.multiple_of(r * CONV_ROWS, CONV_ROWS)
        for c in range(ext.shape[1] // CONV_LANES):
            lanes = slice(c * CONV_LANES, (c + 1) * CONV_LANES)
            win = ext[pl.ds(r0, CONV_ROWS + 2 * CONV_HALO), lanes]
            part = jnp.zeros((CONV_ROWS, CONV_LANES), F32) + dwb_ref[:, lanes]
            for j in range(width):
                o = CONV_HALO - pad + j
                part = part + win[o:o + CONV_ROWS, :] * dw_ref[j:j + 1, lanes]
            conv[:, lanes] = part
        acc = conv[...]
        mu = jnp.mean(acc, axis=-1, keepdims=True)
        xc = acc - mu
        var = jnp.mean(xc * xc, axis=-1, keepdims=True)
        y = xc * lax.rsqrt(var + EPS) * g_ref[...] + b_ref[...]
        o_ref[pl.ds(r0, CONV_ROWS), :] = (y * jax.nn.sigmoid(y)).astype(o_ref.dtype)
        return carry

    lax.fori_loop(0, ts // CONV_ROWS, body, 0)


def _conv_ln_silu(h, dw, dw_b, ln_g, ln_b):
    b, s, d = h.shape
    width = dw.shape[0]
    assert (width - 1) // 2 <= CONV_HALO
    ts = _tile(s, 256)
    nh = ts // CONV_HALO
    n_halo = s // CONV_HALO
    vec = lambda x: x.reshape(1, d)
    vspec = pl.BlockSpec((1, d), lambda bi, i: (0, 0))
    return pl.pallas_call(
        functools.partial(_conv_kernel, width=width),
        grid=(b, s // ts),
        in_specs=[pl.BlockSpec((None, CONV_HALO, d), lambda bi, i: (bi, jnp.maximum(i * nh - 1, 0), 0)),
                  pl.BlockSpec((None, ts, d), lambda bi, i: (bi, i, 0)),
                  pl.BlockSpec((None, CONV_HALO, d), lambda bi, i: (bi, jnp.minimum((i + 1) * nh, n_halo - 1), 0)),
                  pl.BlockSpec((width, d), lambda bi, i: (0, 0)),
                  vspec, vspec, vspec],
        out_specs=pl.BlockSpec((None, ts, d), lambda bi, i: (bi, i, 0)),
        out_shape=jax.ShapeDtypeStruct((b, s, d), BF16),
        scratch_shapes=[pltpu.VMEM((ts + 2 * CONV_HALO, d), F32), pltpu.VMEM((CONV_ROWS, d), F32)],
        compiler_params=_params(),
        name="conv_ln_silu",
    )(h, h, h, dw, vec(dw_b), vec(ln_g), vec(ln_b))


def _split_bf16(x):
    hi = x.astype(BF16)
    return hi, (x - hi.astype(F32)).astype(BF16)


def _router_kernel(h_ref, g_ref, sh_ref, sc_ref, wr_ref, br_ref, v_ref, tw_ref, ti_ref):
    v = _rms(h_ref[0]) * g_ref[...] * (1.0 + sc_ref[...]) + sh_ref[...]
    v_ref[0] = v
    v_hi, v_lo = _split_bf16(v)
    w_hi, w_lo = _split_bf16(wr_ref[...])
    logits = (jnp.dot(v_hi, w_hi, preferred_element_type=F32)
              + jnp.dot(v_lo, w_hi, preferred_element_type=F32)
              + jnp.dot(v_hi, w_lo, preferred_element_type=F32)) + br_ref[...]
    n_exp = logits.shape[1]
    col = lax.broadcasted_iota(jnp.int32, logits.shape, 1).astype(F32)
    work = logits
    vals, idxs = [], []
    for _ in range(TOP_K):
        m = jnp.max(work, axis=-1, keepdims=True)
        idx = jnp.min(jnp.where(work == m, col, float(n_exp)), axis=-1, keepdims=True)
        vals.append(m)
        idxs.append(idx)
        work = jnp.where(col == idx, -jnp.inf, work)
    exps = [jnp.exp(x - vals[0]) for x in vals]
    denom = exps[0]
    for e in exps[1:]:
        denom = denom + e
    tw = jnp.zeros(logits.shape, F32)
    ti = jnp.zeros(logits.shape, F32)
    for k in range(TOP_K):
        tw = jnp.where(col == float(k), exps[k] / denom, tw)
        ti = jnp.where(col == float(k), idxs[k], ti)
    tw_ref[0] = tw
    ti_ref[0] = ti.astype(jnp.int32)


def _router(h, g, ml, row_of_batch, w_r, b_r):
    b, l, d = h.shape
    n_exp = w_r.shape[1]
    tr = _tile(l, 256)
    blk = lambda w: pl.BlockSpec((1, tr, w), lambda i, j: (i, j, 0))
    return pl.pallas_call(
        _router_kernel,
        grid=(b, l // tr),
        in_specs=[blk(d),
                  pl.BlockSpec((1, d), lambda i, j: (0, 0)),
                  _mod_spec(3, d, lambda i, j: row_of_batch(i), lambda i, j: 0),
                  _mod_spec(4, d, lambda i, j: row_of_batch(i), lambda i, j: 0),
                  pl.BlockSpec((d, n_exp), lambda i, j: (0, 0)),
                  pl.BlockSpec((1, n_exp), lambda i, j: (0, 0))],
        out_specs=[blk(d), blk(n_exp), blk(n_exp)],
        out_shape=[jax.ShapeDtypeStruct((b, l, d), F32),
                   jax.ShapeDtypeStruct((b, l, n_exp), F32),
                   jax.ShapeDtypeStruct((b, l, n_exp), jnp.int32)],
        compiler_params=_params(),
        name="router",
    )(h, g.reshape(1, d), ml, ml, w_r, b_r.reshape(1, n_exp))


EXPERT_TILE = 256
DISPATCH_TILE = 256


def _route_tables(top_i, n_exp):
    pairs = top_i.reshape(-1)
    n_pairs = pairs.shape[0]
    onehot = (pairs[:, None] == jnp.arange(n_exp, dtype=jnp.int32)[None, :]).astype(jnp.int32)
    csum = jnp.cumsum(onehot, axis=0)
    counts = csum[-1]
    padded = (counts + EXPERT_TILE - 1) // EXPERT_TILE * EXPERT_TILE
    ends = jnp.cumsum(padded)
    starts = ends - padded
    pos = jnp.sum(onehot * (csum - 1 + starts[None, :]), axis=1).astype(jnp.int32)
    n_tiles = (n_pairs + n_exp * (EXPERT_TILE - 1)) // EXPERT_TILE + 1
    tile_start = jnp.arange(n_tiles, dtype=jnp.int32) * EXPERT_TILE
    tile_e = jnp.minimum(jnp.searchsorted(ends, tile_start, side="right"), n_exp - 1).astype(jnp.int32)
    tile_valid = (tile_start < ends[-1]).astype(jnp.int32)
    pad = jnp.stack([starts + counts, ends]).astype(jnp.int32)
    return pos, tile_e, tile_valid, n_tiles, pad


def _row_copy(src, src_row, dst, dst_row, sem):
    return pltpu.make_async_copy(src.at[pl.ds(src_row, 1), :], dst.at[pl.ds(dst_row, 1), :], sem)


def _dispatch_kernel(pos_ref, pad_ref, v_ref, xs_ref, zeros, sem, zsem):
    i = pl.program_id(0)
    tt = v_ref.shape[0]
    base = i * (tt * TOP_K)

    def issue(r, carry):
        for k in range(TOP_K):
            _row_copy(v_ref, r, xs_ref, pos_ref[base + r * TOP_K + k], sem).start()
        return carry

    lax.fori_loop(0, tt, issue, 0)

    @pl.when(i == 0)
    def _():
        zeros[...] = jnp.zeros(zeros.shape, zeros.dtype)
        n_exp = pad_ref.shape[1]
        zt = zeros.shape[0]
        n_tiles = xs_ref.shape[0] // zt
        first_free = pad_ref[1, n_exp - 1] // zt

        def tile_copy(t):
            return pltpu.make_async_copy(zeros, xs_ref.at[pl.ds(pl.multiple_of(t * zt, zt), zt), :], zsem)

        def for_each_gap(row_fn, tile_fn):
            def per_expert(e, carry):
                lax.fori_loop(pad_ref[0, e], pad_ref[1, e], lambda r, c: (row_fn(r), c)[1], 0)
                return carry
            lax.fori_loop(0, n_exp, per_expert, 0)
            lax.fori_loop(first_free, n_tiles, lambda t, c: (tile_fn(t), c)[1], 0)

        for_each_gap(lambda r: _row_copy(zeros, 0, xs_ref, r, zsem).start(), lambda t: tile_copy(t).start())
        for_each_gap(lambda r: _row_copy(zeros, 0, xs_ref, r, zsem).wait(), lambda t: tile_copy(t).wait())

    def drain(r, carry):
        for k in range(TOP_K):
            _row_copy(v_ref, r, xs_ref, pos_ref[base + r * TOP_K + k], sem).wait()
        return carry

    lax.fori_loop(0, tt, drain, 0)


def _dispatch(v, pos, pad, n_rows):
    t, d = v.shape
    tt = _tile(t, DISPATCH_TILE)
    return pl.pallas_call(
        _dispatch_kernel,
        grid_spec=pltpu.PrefetchScalarGridSpec(
            num_scalar_prefetch=2,
            grid=(t // tt,),
            in_specs=[pl.BlockSpec((tt, d), lambda i, pos_ref, pad_ref: (i, 0))],
            out_specs=pl.BlockSpec(memory_space=pl.ANY),
            scratch_shapes=[pltpu.VMEM((EXPERT_TILE, d), F32),
                            pltpu.SemaphoreType.DMA(()), pltpu.SemaphoreType.DMA(())]),
        out_shape=jax.ShapeDtypeStruct((n_rows, d), F32),
        compiler_params=_params(),
        name="moe_dispatch",
    )(pos, pad, v)


def _experts_kernel(te_ref, tv_ref, x_ref, wgu_ref, bgu_ref, wdn_ref, bdn_ref, y_ref):
    i = pl.program_id(0)

    @pl.when(tv_ref[i] > 0)
    def _():
        de = wdn_ref.shape[1]
        hgu = jnp.dot(x_ref[...].astype(BF16), wgu_ref[0], preferred_element_type=F32) + bgu_ref[0]
        g_lin = jnp.minimum(hgu[:, :de], SWIGLU_LIMIT)
        up = jnp.clip(hgu[:, de:], -SWIGLU_LIMIT, SWIGLU_LIMIT)
        act = g_lin * jax.nn.sigmoid(SWIGLU_ALPHA * g_lin) * (up + 1.0)
        y_ref[...] = jnp.dot(act.astype(BF16), wdn_ref[0], preferred_element_type=F32) + bdn_ref[0]

    @pl.when(tv_ref[i] == 0)
    def _():
        y_ref[...] = jnp.zeros(y_ref.shape, y_ref.dtype)


def _experts(xs, tile_e, tile_valid, w_gu, b_gu, w_dn, b_dn):
    n_rows, d = xs.shape
    n_exp, _, two_de = w_gu.shape
    de = two_de // 2
    n_tiles = n_rows // EXPERT_TILE
    return pl.pallas_call(
        _experts_kernel,
        grid_spec=pltpu.PrefetchScalarGridSpec(
            num_scalar_prefetch=2,
            grid=(n_tiles,),
            in_specs=[pl.BlockSpec((EXPERT_TILE, d), lambda i, te, tv: (i, 0)),
                      pl.BlockSpec((1, d, two_de), lambda i, te, tv: (te[i], 0, 0)),
                      pl.BlockSpec((1, 1, two_de), lambda i, te, tv: (te[i], 0, 0)),
                      pl.BlockSpec((1, de, d), lambda i, te, tv: (te[i], 0, 0)),
                      pl.BlockSpec((1, 1, d), lambda i, te, tv: (te[i], 0, 0))],
            out_specs=pl.BlockSpec((EXPERT_TILE, d), lambda i, te, tv: (i, 0))),
        out_shape=jax.ShapeDtypeStruct((n_rows, d), F32),
        compiler_params=_params(),
        name="moe_experts",
    )(tile_e, tile_valid, xs, w_gu, b_gu.reshape(n_exp, 1, two_de), w_dn, b_dn.reshape(n_exp, 1, d))


COMBINE_TILE = 128


def _combine_kernel(pos_ref, ys_ref, tw_ref, gate_ref, h_ref, o_ref, buf, sem):
    i = pl.program_id(0) * pl.num_programs(1) + pl.program_id(1)
    tc = h_ref.shape[1]
    base = i * (tc * TOP_K)

    def issue(r, carry):
        for k in range(TOP_K):
            _row_copy(ys_ref, pos_ref[base + r * TOP_K + k], buf.at[k], r, sem).start()
        return carry

    lax.fori_loop(0, tc, issue, 0)

    def drain(r, carry):
        for k in range(TOP_K):
            _row_copy(ys_ref, pos_ref[base + r * TOP_K + k], buf.at[k], r, sem).wait()
        return carry

    lax.fori_loop(0, tc, drain, 0)
    tw = tw_ref[0]
    f = buf[0] * tw[:, 0:1]
    for k in range(1, TOP_K):
        f = f + buf[k] * tw[:, k:k + 1]
    o_ref[0] = h_ref[0] + gate_ref[...] * f


def _combine(ys, pos, top_w, h, ml, row_of_batch):
    b, l, d = h.shape
    n_exp = top_w.shape[-1]
    tc = _tile(l, COMBINE_TILE)
    return pl.pallas_call(
        _combine_kernel,
        grid_spec=pltpu.PrefetchScalarGridSpec(
            num_scalar_prefetch=1,
            grid=(b, l // tc),
            in_specs=[pl.BlockSpec(memory_space=pl.ANY),
                      pl.BlockSpec((1, tc, n_exp), lambda bi, i, p: (bi, i, 0)),
                      _mod_spec(5, d, lambda bi, i, p: row_of_batch(bi), lambda bi, i, p: 0),
                      pl.BlockSpec((1, tc, d), lambda bi, i, p: (bi, i, 0))],
            out_specs=pl.BlockSpec((1, tc, d), lambda bi, i, p: (bi, i, 0)),
            scratch_shapes=[pltpu.VMEM((TOP_K, tc, d), F32), pltpu.SemaphoreType.DMA(())]),
        out_shape=jax.ShapeDtypeStruct((b, l, d), F32),
        input_output_aliases={4: 0},
        compiler_params=_params(),
        name="moe_combine",
    )(pos, ys, top_w, ml, h)


def _moe(h, g, ml, row_of_batch, w_r, b_r, w_gu, b_gu, w_dn, b_dn):
    b, l, d = h.shape
    n_exp = w_r.shape[1]
    v, top_w, top_i = _router(h, g, ml, row_of_batch, w_r, b_r)
    pos, tile_e, tile_valid, n_tiles, pad = _route_tables(top_i[..., :TOP_K], n_exp)
    xs = _dispatch(v.reshape(b * l, d), pos, pad, n_tiles * EXPERT_TILE)
    ys = _experts(xs, tile_e, tile_valid, w_gu, b_gu, w_dn, b_dn)
    return _combine(ys, pos, top_w, h, ml, row_of_batch)


def kernel(x, c, ctx, c_ctx, norm1_g, norm2_g, final_g, mod_a, mod_b, mod_bias, w_o, b_o, diff_w_in, diff_lambda, diff_subln_g, gqa_w_in, gqa_q_norm, gqa_k_norm, conv_pw1, conv_pw1_b, conv_dw, conv_dw_b, conv_ln_g, conv_ln_b, router_w, router_b, exp_w_gu, exp_b_gu, exp_w_dn, exp_b_dn):
    b, s, d = x.shape
    lc = ctx.shape[1]
    depth = mod_a.shape[0]
    rows = s // GRID_W
    cos, sin = _rope_tables(rows)
    ctx_last = max([i for i in range(depth) if i % N_MIXERS in ATTN_KINDS], default=-1)

    n_rows = -(-(b + 1) // SUBLANE) * SUBLANE
    cond = jnp.zeros((n_rows, d), F32).at[:b].set(c).at[b].set(c_ctx)
    m_all = _adaln(cond, mod_a, mod_b, mod_bias).reshape(depth, n_rows, 6, 1, d)
    lat_row = lambda bi: bi
    ctx_row = lambda bi: b

    h_lat, h_ctx = x, ctx
    for i in range(depth):
        kind = i % N_MIXERS
        j = i // N_MIXERS
        upd_ctx = i < ctx_last
        need_ctx = upd_ctx or kind in ATTN_KINDS
        ml = m_all[i]
        w_o_i = w_o[i].astype(BF16)

        u_lat = _norm_mod(h_lat, norm1_g[i], ml, 0, lat_row)
        u_ctx = _norm_mod(h_ctx, norm1_g[i], ml, 0, ctx_row) if need_ctx else None
        o_ctx = None
        if kind == 0:
            o_lat = _fourier_mix(u_lat)
            if upd_ctx:
                o_ctx = _fourier_mix(u_ctx)
        elif kind == 1:
            lam_init = 0.8 - 0.6 * math.exp(-0.3 * i)
            w_in = diff_w_in[j].astype(BF16)
            qkv_lat = _qkv(u_lat.reshape(b * s, d), w_in, cos, sin, d, d, True).reshape(b, s, 3 * d)
            qkv_ctx = _qkv(u_ctx.reshape(b * lc, d), w_in, cos, sin, d, d, False).reshape(b, lc, 3 * d)
            o_lat = _attn_diff(qkv_lat, [qkv_ctx, qkv_lat], diff_lambda[j], diff_subln_g[j], lam_init, d)
            if upd_ctx:
                o_ctx = _attn_diff(qkv_ctx, [qkv_ctx], diff_lambda[j], diff_subln_g[j], lam_init, d)
        elif kind == 2:
            n_heads = d // HEAD_DIM
            nq, nk = n_heads * HEAD_DIM, n_heads // GQA_GROUP * HEAD_DIM
            w_in = gqa_w_in[j].astype(BF16)
            gains = (gqa_q_norm[j], gqa_k_norm[j])
            qkv_lat = _qkv(u_lat.reshape(b * s, d), w_in, cos, sin, nq, nk, True, gains).reshape(b, s, -1)
            qkv_ctx = _qkv(u_ctx.reshape(b * lc, d), w_in, cos, sin, nq, nk, False, gains).reshape(b, lc, -1)
            o_lat = _attn_gqa(qkv_lat, [qkv_ctx, qkv_lat], n_heads)
            if upd_ctx:
                o_ctx = _attn_gqa(qkv_ctx, [qkv_ctx], n_heads)
        else:
            pw1 = conv_pw1[j].astype(BF16)

            def conv(u):
                bb, ll, _ = u.shape
                hg = _linear_glu(u.reshape(bb * ll, d), pw1, conv_pw1_b[j]).reshape(bb, ll, d)
                return _conv_ln_silu(hg, conv_dw[j], conv_dw_b[j], conv_ln_g[j], conv_ln_b[j])

            o_lat = conv(u_lat)
            if upd_ctx:
                o_ctx = conv(u_ctx)

        moe_w = (router_w[i], router_b[i], exp_w_gu[i].astype(BF16), exp_b_gu[i],
                 exp_w_dn[i].astype(BF16), exp_b_dn[i])
        h_lat = _linear_residual(o_lat, w_o_i, b_o[i], h_lat, ml, 2, lat_row)
        if upd_ctx:
            h_ctx = _linear_residual(o_ctx, w_o_i, b_o[i], h_ctx, ml, 2, ctx_row)
            h_ctx = _moe(h_ctx, norm2_g[i], ml, ctx_row, *moe_w)
        h_lat = _moe(h_lat, norm2_g[i], ml, lat_row, *moe_w)

    return _final_norm(h_lat, final_g)
```

```python
import functools
import math

import jax
import jax.numpy as jnp
from jax import lax
from jax.experimental import pallas as pl
from jax.experimental.pallas import tpu as pltpu

F32 = jnp.float32
BF16 = jnp.bfloat16

EPS = 1e-6
HEAD_DIM = 128
ROPE_THETA = 10000.0
GRID_W = 64
FOURIER_GROUPS = 8
GQA_GROUP = 4
TOP_K = 4
SWIGLU_LIMIT = 7.0
SWIGLU_ALPHA = 1.702
N_MIXERS = 4
ATTN_KINDS = (1, 2)

V7X_VMEM_BYTES = 64 * 1024 * 1024
VMEM_LIMIT = V7X_VMEM_BYTES - 8 * 1024 * 1024
LANE = 128
SUBLANE = 8


def _params(*sem):
    return pltpu.CompilerParams(dimension_semantics=sem if sem else None,
                                vmem_limit_bytes=VMEM_LIMIT)


def _tile(n, pref):
    t = min(n, pref)
    assert n % t == 0, (n, pref)
    return t


def _adaln_kernel(c_ref, a_ref, b_ref, bias_ref, o_ref):
    c = c_ref[...]
    s = c * jax.nn.sigmoid(c)
    t = jnp.dot(s, a_ref[0], preferred_element_type=F32)
    o_ref[0] = jnp.dot(t, b_ref[0], preferred_element_type=F32) + bias_ref[0]


def _adaln(cond, mod_a, mod_b, mod_bias):
    depth, d, rank = mod_a.shape
    n = mod_b.shape[-1]
    r = cond.shape[0]
    tn = _tile(n, 2048)
    return pl.pallas_call(
        _adaln_kernel,
        grid=(depth, n // tn),
        in_specs=[pl.BlockSpec((r, d), lambda i, j: (0, 0)),
                  pl.BlockSpec((1, d, rank), lambda i, j: (i, 0, 0)),
                  pl.BlockSpec((1, rank, tn), lambda i, j: (i, 0, j)),
                  pl.BlockSpec((1, 1, tn), lambda i, j: (i, 0, j))],
        out_specs=pl.BlockSpec((1, r, tn), lambda i, j: (i, 0, j)),
        out_shape=jax.ShapeDtypeStruct((depth, r, n), F32),
        compiler_params=_params(),
        name="adaln",
    )(cond, mod_a, mod_b, mod_bias.reshape(depth, 1, n))


def _mod_spec(chunk, width, row_fn, col_fn):
    return pl.BlockSpec((None, None, 1, width),
                        lambda *idx: (row_fn(*idx), chunk, 0, col_fn(*idx)))


def _rms(x):
    return x * lax.rsqrt(jnp.mean(x * x, axis=-1, keepdims=True) + EPS)


def _norm_mod_kernel(h_ref, g_ref, sh_ref, sc_ref, o_ref):
    y = _rms(h_ref[0]) * g_ref[...]
    o_ref[0] = (y * (1.0 + sc_ref[...]) + sh_ref[...]).astype(o_ref.dtype)


def _norm_mod(h, g, ml, chunk0, row_of_batch):
    b, l, d = h.shape
    tr = _tile(l, 256)
    return pl.pallas_call(
        _norm_mod_kernel,
        grid=(b, l // tr),
        in_specs=[pl.BlockSpec((1, tr, d), lambda i, j: (i, j, 0)),
                  pl.BlockSpec((1, d), lambda i, j: (0, 0)),
                  _mod_spec(chunk0, d, lambda i, j: row_of_batch(i), lambda i, j: 0),
                  _mod_spec(chunk0 + 1, d, lambda i, j: row_of_batch(i), lambda i, j: 0)],
        out_specs=pl.BlockSpec((1, tr, d), lambda i, j: (i, j, 0)),
        out_shape=jax.ShapeDtypeStruct((b, l, d), BF16),
        compiler_params=_params(),
        name="norm_mod",
    )(h, g.reshape(1, d), ml, ml)


def _final_norm_kernel(h_ref, g_ref, o_ref):
    o_ref[0] = _rms(h_ref[0]) * g_ref[...]


def _final_norm(h, g):
    b, l, d = h.shape
    tr = _tile(l, 256)
    return pl.pallas_call(
        _final_norm_kernel,
        grid=(b, l // tr),
        in_specs=[pl.BlockSpec((1, tr, d), lambda i, j: (i, j, 0)),
                  pl.BlockSpec((1, d), lambda i, j: (0, 0))],
        out_specs=pl.BlockSpec((1, tr, d), lambda i, j: (i, j, 0)),
        out_shape=jax.ShapeDtypeStruct((b, l, d), F32),
        compiler_params=_params(),
        name="final_norm",
    )(h, g.reshape(1, d))


def _linear_kernel(a_ref, w_ref, o_ref):
    o_ref[0] = jnp.dot(a_ref[...], w_ref[0], preferred_element_type=F32).astype(o_ref.dtype)


def _linear(a, w, out_dtype, tm=512, tn=512):
    m, k = a.shape
    g, _, n = w.shape
    tm, tn = _tile(m, tm), _tile(n, tn)
    return pl.pallas_call(
        _linear_kernel,
        grid=(g, m // tm, n // tn),
        in_specs=[pl.BlockSpec((tm, k), lambda b, i, j: (i, 0)),
                  pl.BlockSpec((1, k, tn), lambda b, i, j: (b, 0, j))],
        out_specs=pl.BlockSpec((1, tm, tn), lambda b, i, j: (b, i, j)),
        out_shape=jax.ShapeDtypeStruct((g, m, n), out_dtype),
        compiler_params=_params(),
        name="linear",
    )(a, w)


def _linear_residual_kernel(a_ref, w_ref, b_ref, gate_ref, h_ref, o_ref):
    acc = jnp.dot(a_ref[0], w_ref[...], preferred_element_type=F32) + b_ref[...]
    o_ref[0] = h_ref[0] + gate_ref[...] * acc


def _linear_residual(o, w, bias, h, ml, gate_chunk, row_of_batch):
    b, l, k = o.shape
    n = w.shape[1]
    tm, tn = _tile(l, 1024), _tile(n, 512)
    return pl.pallas_call(
        _linear_residual_kernel,
        grid=(b, l // tm, n // tn),
        in_specs=[pl.BlockSpec((1, tm, k), lambda bi, i, j: (bi, i, 0)),
                  pl.BlockSpec((k, tn), lambda bi, i, j: (0, j)),
                  pl.BlockSpec((1, tn), lambda bi, i, j: (0, j)),
                  _mod_spec(gate_chunk, tn, lambda bi, i, j: row_of_batch(bi), lambda bi, i, j: j),
                  pl.BlockSpec((1, tm, tn), lambda bi, i, j: (bi, i, j))],
        out_specs=pl.BlockSpec((1, tm, tn), lambda bi, i, j: (bi, i, j)),
        out_shape=jax.ShapeDtypeStruct((b, l, n), F32),
        input_output_aliases={4: 0},
        compiler_params=_params(),
        name="linear_residual",
    )(o, w, bias.reshape(1, n), ml, h)


def _linear_glu_kernel(a_ref, w1_ref, w2_ref, b1_ref, b2_ref, o_ref):
    a = a_ref[...]
    lin = jnp.dot(a, w1_ref[...], preferred_element_type=F32) + b1_ref[...]
    gate = jnp.dot(a, w2_ref[...], preferred_element_type=F32) + b2_ref[...]
    o_ref[...] = (lin * jax.nn.sigmoid(gate)).astype(o_ref.dtype)


def _linear_glu(a, w, bias):
    m, k = a.shape
    n = w.shape[1] // 2
    tm, tn = _tile(m, 1024), _tile(n, 256)
    nj = n // tn
    bias = bias.reshape(1, 2 * n)
    return pl.pallas_call(
        _linear_glu_kernel,
        grid=(m // tm, nj),
        in_specs=[pl.BlockSpec((tm, k), lambda i, j: (i, 0)),
                  pl.BlockSpec((k, tn), lambda i, j: (0, j)),
                  pl.BlockSpec((k, tn), lambda i, j: (0, j + nj)),
                  pl.BlockSpec((1, tn), lambda i, j: (0, j)),
                  pl.BlockSpec((1, tn), lambda i, j: (0, j + nj))],
        out_specs=pl.BlockSpec((tm, tn), lambda i, j: (i, j)),
        out_shape=jax.ShapeDtypeStruct((m, n), BF16),
        compiler_params=_params(),
        name="linear_glu",
    )(a, w, w, bias, bias)


def _rope_tables(rows):
    nf = HEAD_DIM // 4
    row = jnp.repeat(jnp.arange(rows, dtype=F32), GRID_W)
    col = jnp.tile(jnp.arange(GRID_W, dtype=F32), rows)
    inv = ROPE_THETA ** (-jnp.arange(nf, dtype=F32) / nf)
    ar, ac = row[:, None] * inv, col[:, None] * inv
    cos = jnp.concatenate([jnp.cos(ar), jnp.cos(ar), jnp.cos(ac), jnp.cos(ac)], axis=1)
    sin = jnp.concatenate([-jnp.sin(ar), jnp.sin(ar), -jnp.sin(ac), jnp.sin(ac)], axis=1)
    return cos, sin


def _rope(x, cos, sin):
    nf = HEAD_DIM // 4
    lane = lax.broadcasted_iota(jnp.int32, x.shape, 1)
    first = (lane % (2 * nf)) < nf
    partner = jnp.where(first, pltpu.roll(x, HEAD_DIM - nf, 1), pltpu.roll(x, nf, 1))
    return x * cos + partner * sin


def _qkv_diff_kernel(a_ref, w_ref, cos_ref, sin_ref, o_ref, *, n_q, n_k, rope, scale):
    j = pl.program_id(1)
    acc = jnp.dot(a_ref[...], w_ref[...], preferred_element_type=F32)
    tn = acc.shape[1]

    def store(mult):
        for c in range(tn // HEAD_DIM):
            x = acc[:, c * HEAD_DIM:(c + 1) * HEAD_DIM]
            if rope:
                x = _rope(x, cos_ref[...], sin_ref[...])
            o_ref[:, c * HEAD_DIM:(c + 1) * HEAD_DIM] = (x * mult).astype(o_ref.dtype)

    @pl.when(j < n_q)
    def _():
        store(scale)

    @pl.when(jnp.logical_and(j >= n_q, j < n_q + n_k))
    def _():
        store(1.0)

    @pl.when(j >= n_q + n_k)
    def _():
        o_ref[...] = acc.astype(o_ref.dtype)


def _qkv_gqa_kernel(a_ref, w_ref, cos_ref, sin_ref, qg_ref, kg_ref, o_ref, *, n_q, n_k, rope, scale):
    j = pl.program_id(1)
    acc = jnp.dot(a_ref[...], w_ref[...], preferred_element_type=F32)
    tn = acc.shape[1]

    def store(g_ref, mult):
        for c in range(tn // HEAD_DIM):
            x = _rms(acc[:, c * HEAD_DIM:(c + 1) * HEAD_DIM]) * g_ref[...]
            if rope:
                x = _rope(x, cos_ref[...], sin_ref[...])
            o_ref[:, c * HEAD_DIM:(c + 1) * HEAD_DIM] = (x * mult).astype(o_ref.dtype)

    @pl.when(j < n_q)
    def _():
        store(qg_ref, scale)

    @pl.when(jnp.logical_and(j >= n_q, j < n_q + n_k))
    def _():
        store(kg_ref, 1.0)

    @pl.when(j >= n_q + n_k)
    def _():
        o_ref[...] = acc.astype(o_ref.dtype)


def _qkv(a, w, cos, sin, n_q_cols, n_k_cols, rope, gains=None):
    m, k = a.shape
    n = w.shape[1]
    s = cos.shape[0]
    tm, tn = _tile(min(m, s), 1024), _tile(n, 512)
    assert n_q_cols % tn == 0 and n_k_cols % tn == 0
    npos = s // tm
    kw = dict(n_q=n_q_cols // tn, n_k=n_k_cols // tn, rope=rope, scale=HEAD_DIM ** -0.5 * math.log2(math.e))
    in_specs = [pl.BlockSpec((tm, k), lambda i, j: (i, 0)),
                pl.BlockSpec((k, tn), lambda i, j: (0, j)),
                pl.BlockSpec((tm, HEAD_DIM), lambda i, j: (i % npos, 0)),
                pl.BlockSpec((tm, HEAD_DIM), lambda i, j: (i % npos, 0))]
    args = [a, w, cos, sin]
    if gains is None:
        kernel = functools.partial(_qkv_diff_kernel, **kw)
    else:
        kernel = functools.partial(_qkv_gqa_kernel, **kw)
        in_specs += [pl.BlockSpec((1, HEAD_DIM), lambda i, j: (0, 0))] * 2
        args += [gains[0].reshape(1, HEAD_DIM), gains[1].reshape(1, HEAD_DIM)]
    return pl.pallas_call(
        kernel,
        grid=(m // tm, n // tn),
        in_specs=in_specs,
        out_specs=pl.BlockSpec((tm, tn), lambda i, j: (i, j)),
        out_shape=jax.ShapeDtypeStruct((m, n), BF16),
        compiler_params=_params(),
        name="qkv",
    )(*args)


def _dft_cos_sin(n):
    k = jnp.arange(n, dtype=jnp.int32)
    ang = ((k[:, None] * k[None, :]) % n).astype(F32) * (2.0 * math.pi / n)
    s = 1.0 / math.sqrt(n)
    return jnp.cos(ang) * s, jnp.sin(ang) * s


def _fourier_channels_kernel(u_ref, w_ref, o_ref, *, groups):
    cg = w_ref.shape[0]
    for g in range(groups):
        x = u_ref[0, :, g * cg:(g + 1) * cg]
        y = jnp.dot(x, w_ref[...], preferred_element_type=F32)
        o_ref[0, 0, :, g * cg:(g + 1) * cg] = y[:, :cg].astype(o_ref.dtype)
        o_ref[0, 1, :, g * cg:(g + 1) * cg] = y[:, cg:].astype(o_ref.dtype)


def _fourier_mix(u):
    b, l, d = u.shape
    cg = d // FOURIER_GROUPS
    cc, sc = _dft_cos_sin(cg)
    wch = jnp.concatenate([cc, -sc], axis=1).astype(BF16)
    tm = _tile(l, 512)
    y = pl.pallas_call(
        functools.partial(_fourier_channels_kernel, groups=FOURIER_GROUPS),
        grid=(b, l // tm),
        in_specs=[pl.BlockSpec((1, tm, d), lambda i, j: (i, j, 0)),
                  pl.BlockSpec((cg, 2 * cg), lambda i, j: (0, 0))],
        out_specs=pl.BlockSpec((1, 2, tm, d), lambda i, j: (i, 0, j, 0)),
        out_shape=jax.ShapeDtypeStruct((b, 2, l, d), BF16),
        compiler_params=_params(),
        name="fourier_channels",
    )(u, wch)
    cl, sl = _dft_cos_sin(l)
    wpos = jnp.concatenate([cl, sl], axis=1).astype(BF16)
    return _linear(wpos, y.reshape(b, 2 * l, d), BF16)


def _softmax_parts(s):
    m = jnp.max(s, axis=-1, keepdims=True)
    p = jnp.exp2(s - m)
    return p, jnp.sum(p, axis=-1, keepdims=True)


def _nt_dot(a, b):
    return lax.dot_general(a, b, (((1,), (1,)), ((), ())), preferred_element_type=F32)


def _attn_diff_kernel(*refs, n_src, tq, lam_init):
    lam_ref, g_ref, q_ref = refs[:3]
    kv_refs = refs[3:3 + 2 * n_src]
    o_ref, kbuf, vbuf = refs[3 + 2 * n_src:]
    off = 0
    for s in range(n_src):
        k_ref, v_ref = kv_refs[2 * s], kv_refs[2 * s + 1]
        n = k_ref.shape[0]
        kbuf[off:off + n, :] = k_ref[...]
        vbuf[off:off + n, :] = v_ref[...]
        off += n
    lp = lam_ref[...]
    lam = (jnp.exp(jnp.sum(lp[0:1] * lp[1:2], axis=-1, keepdims=True))
           - jnp.exp(jnp.sum(lp[2:3] * lp[3:4], axis=-1, keepdims=True)) + lam_init)

    def body(qi, carry):
        r0 = pl.multiple_of(qi * tq, tq)
        q = q_ref[pl.ds(r0, tq), :]
        p0, l0 = _softmax_parts(_nt_dot(q[:, :HEAD_DIM], kbuf[:, :HEAD_DIM]))
        p1, l1 = _softmax_parts(_nt_dot(q[:, HEAD_DIM:], kbuf[:, HEAD_DIM:]))
        a = p0 * (1.0 / l0) - p1 * (lam / l1)
        o = jnp.dot(a.astype(BF16), vbuf[...], preferred_element_type=F32)
        o = _rms(o) * g_ref[...] * (1.0 - lam_init)
        o_ref[pl.ds(r0, tq), :] = o.astype(o_ref.dtype)
        return carry

    n_q = q_ref.shape[0] // tq
    lax.fori_loop(0, n_q, body, 0, unroll=2 if n_q % 2 == 0 else 1)


def _attn_diff(q_src, kv_srcs, lam_p, subln_g, lam_init, d_model):
    b, lq, _ = q_src.shape
    hw = 2 * HEAD_DIM
    nh = d_model // hw
    lk = sum(x.shape[1] for x in kv_srcs)
    tq = _tile(lq, 256)
    in_specs = [pl.BlockSpec((4, HEAD_DIM), lambda bi, h: (0, 0)),
                pl.BlockSpec((1, hw), lambda bi, h: (0, 0)),
                pl.BlockSpec((None, lq, hw), lambda bi, h: (bi, 0, h))]
    args = [lam_p, subln_g.reshape(1, hw), q_src]
    for x in kv_srcs:
        in_specs += [pl.BlockSpec((None, x.shape[1], hw), lambda bi, h: (bi, 0, nh + h)),
                     pl.BlockSpec((None, x.shape[1], hw), lambda bi, h: (bi, 0, 2 * nh + h))]
        args += [x, x]
    return pl.pallas_call(
        functools.partial(_attn_diff_kernel, n_src=len(kv_srcs), tq=tq, lam_init=lam_init),
        grid=(b, nh),
        in_specs=in_specs,
        out_specs=pl.BlockSpec((None, lq, hw), lambda bi, h: (bi, 0, h)),
        out_shape=jax.ShapeDtypeStruct((b, lq, d_model), BF16),
        scratch_shapes=[pltpu.VMEM((lk, hw), BF16), pltpu.VMEM((lk, hw), BF16)],
        compiler_params=_params(),
        name="attn_diff",
    )(*args)


def _attn_gqa_kernel(*refs, n_src, tq, group):
    q_ref = refs[0]
    kv_refs = refs[1:1 + 2 * n_src]
    o_ref, kbuf, vbuf = refs[1 + 2 * n_src:]
    off = 0
    for s in range(n_src):
        k_ref, v_ref = kv_refs[2 * s], kv_refs[2 * s + 1]
        n = k_ref.shape[0]
        kbuf[off:off + n, :] = k_ref[...]
        vbuf[off:off + n, :] = v_ref[...]
        off += n

    def body(qi, carry):
        r0 = pl.multiple_of(qi * tq, tq)
        for hh in range(group):
            q = q_ref[pl.ds(r0, tq), hh * HEAD_DIM:(hh + 1) * HEAD_DIM]
            p, l = _softmax_parts(_nt_dot(q, kbuf[...]))
            o = jnp.dot(p.astype(BF16), vbuf[...], preferred_element_type=F32) * (1.0 / l)
            o_ref[pl.ds(r0, tq), hh * HEAD_DIM:(hh + 1) * HEAD_DIM] = o.astype(o_ref.dtype)
        return carry

    lax.fori_loop(0, q_ref.shape[0] // tq, body, 0)


def _attn_gqa(q_src, kv_srcs, n_heads):
    b, lq, _ = q_src.shape
    n_kv = n_heads // GQA_GROUP
    gw = GQA_GROUP * HEAD_DIM
    lk = sum(x.shape[1] for x in kv_srcs)
    tq = _tile(lq, 256)
    in_specs = [pl.BlockSpec((None, lq, gw), lambda bi, g: (bi, 0, g))]
    args = [q_src]
    for x in kv_srcs:
        in_specs += [pl.BlockSpec((None, x.shape[1], HEAD_DIM), lambda bi, g: (bi, 0, n_heads + g)),
                     pl.BlockSpec((None, x.shape[1], HEAD_DIM), lambda bi, g: (bi, 0, n_heads + n_kv + g))]
        args += [x, x]
    return pl.pallas_call(
        functools.partial(_attn_gqa_kernel, n_src=len(kv_srcs), tq=tq, group=GQA_GROUP),
        grid=(b, n_kv),
        in_specs=in_specs,
        out_specs=pl.BlockSpec((None, lq, gw), lambda bi, g: (bi, 0, g)),
        out_shape=jax.ShapeDtypeStruct((b, lq, n_heads * HEAD_DIM), BF16),
        scratch_shapes=[pltpu.VMEM((lk, HEAD_DIM), BF16), pltpu.VMEM((lk, HEAD_DIM), BF16)],
        compiler_params=_params(),
        name="attn_gqa",
    )(*args)


CONV_HALO = 16
CONV_ROWS = 8
CONV_LANES = 256


def _conv_kernel(prev_ref, cur_ref, next_ref, dw_ref, dwb_ref, g_ref, b_ref, o_ref, ext, conv, taps8, *, width):
    i = pl.program_id(1)
    last = pl.num_programs(1) - 1
    ts = cur_ref.shape[0]
    pad = (width - 1) // 2
    ext[0:CONV_HALO, :] = jnp.where(i > 0, prev_ref[...].astype(F32), 0.0)
    ext[CONV_HALO:CONV_HALO + ts, :] = cur_ref[...].astype(F32)
    ext[CONV_HALO + ts:, :] = jnp.where(i < last, next_ref[...].astype(F32), 0.0)

    first = CONV_HALO - pad
    n_blk = (first + width - 1) // CONV_ROWS + 1
    sub = lax.broadcasted_iota(jnp.int32, (CONV_ROWS, CONV_LANES), 0)
    for j in range(width):
        taps8[j] = jnp.broadcast_to(dw_ref[j:j + 1, :], taps8.shape[1:])

    def residue_sums(m, lanes):
        r0 = pl.multiple_of(m * CONV_ROWS, CONV_ROWS)
        blocks = [ext[pl.ds(r0 + CONV_ROWS * a, CONV_ROWS), lanes] for a in range(n_blk)]
        sums = []
        for b in range(CONV_ROWS):
            taps = [(a, CONV_ROWS * a + b - first) for a in range(n_blk)
                    if 0 <= CONV_ROWS * a + b - first < width]
            acc = blocks[taps[0][0]] * taps8[taps[0][1], :, lanes]
            for a, j in taps[1:]:
                acc = acc + blocks[a] * taps8[j, :, lanes]
            sums.append(acc)
        return tuple(sums)

    for c in range(ext.shape[1] // CONV_LANES):
        lanes = slice(c * CONV_LANES, (c + 1) * CONV_LANES)

        def conv_block(m, cur, lanes=lanes):
            nxt = residue_sums(m + 1, lanes)
            out = cur[0] + dwb_ref[:, lanes]
            for b in range(1, CONV_ROWS):
                mixed = jnp.where(sub >= b, cur[b], nxt[b])
                out = out + pltpu.roll(mixed, CONV_ROWS - b, 0)
            conv[pl.ds(pl.multiple_of(m * CONV_ROWS, CONV_ROWS), CONV_ROWS), lanes] = out
            return nxt

        lax.fori_loop(0, ts // CONV_ROWS, conv_block, residue_sums(0, lanes))

    def body(r, carry):
        r0 = pl.multiple_of(r * CONV_ROWS, CONV_ROWS)
        acc = conv[pl.ds(r0, CONV_ROWS), :]
        mu = jnp.mean(acc, axis=-1, keepdims=True)
        xc = acc - mu
        var = jnp.mean(xc * xc, axis=-1, keepdims=True)
        y = xc * lax.rsqrt(var + EPS) * g_ref[...] + b_ref[...]
        o_ref[pl.ds(r0, CONV_ROWS), :] = (y * jax.nn.sigmoid(y)).astype(o_ref.dtype)
        return carry

    lax.fori_loop(0, ts // CONV_ROWS, body, 0)


def _conv_ln_silu(h, dw, dw_b, ln_g, ln_b):
    b, s, d = h.shape
    width = dw.shape[0]
    assert (width - 1) // 2 <= CONV_HALO
    ts = _tile(s, 256)
    nh = ts // CONV_HALO
    n_halo = s // CONV_HALO
    vec = lambda x: x.reshape(1, d)
    vspec = pl.BlockSpec((1, d), lambda bi, i: (0, 0))
    return pl.pallas_call(
        functools.partial(_conv_kernel, width=width),
        grid=(b, s // ts),
        in_specs=[pl.BlockSpec((None, CONV_HALO, d), lambda bi, i: (bi, jnp.maximum(i * nh - 1, 0), 0)),
                  pl.BlockSpec((None, ts, d), lambda bi, i: (bi, i, 0)),
                  pl.BlockSpec((None, CONV_HALO, d), lambda bi, i: (bi, jnp.minimum((i + 1) * nh, n_halo - 1), 0)),
                  pl.BlockSpec((width, d), lambda bi, i: (0, 0)),
                  vspec, vspec, vspec],
        out_specs=pl.BlockSpec((None, ts, d), lambda bi, i: (bi, i, 0)),
        out_shape=jax.ShapeDtypeStruct((b, s, d), BF16),
        scratch_shapes=[pltpu.VMEM((ts + 2 * CONV_HALO, d), F32), pltpu.VMEM((ts, d), F32),
                        pltpu.VMEM((width, CONV_ROWS, d), F32)],
        compiler_params=_params(),
        name="conv_ln_silu",
    )(h, h, h, dw, vec(dw_b), vec(ln_g), vec(ln_b))


def _split_bf16(x):
    hi = x.astype(BF16)
    return hi, (x - hi.astype(F32)).astype(BF16)


def _pack_pair(lo, hi):
    lo_bits = lax.bitcast_convert_type(lo.astype(BF16).astype(F32), jnp.uint32)
    hi_bits = lax.bitcast_convert_type(hi.astype(BF16).astype(F32), jnp.uint32)
    return (lo_bits >> 16) | (hi_bits & jnp.uint32(0xFFFF0000))


def _unpack_pair(w):
    lo = lax.bitcast_convert_type(w << 16, F32)
    hi = lax.bitcast_convert_type(w & jnp.uint32(0xFFFF0000), F32)
    return lo, hi


def _router_kernel(h_ref, g_ref, sh_ref, sc_ref, wr_ref, br_ref, v_ref, tw_ref, ti_ref, rk_ref, cnt_ref):
    v = _rms(h_ref[0]) * g_ref[...] * (1.0 + sc_ref[...]) + sh_ref[...]
    half = v.shape[1] // 2
    v_ref[0] = _pack_pair(v[:, :half], v[:, half:])
    v_hi, v_lo = _split_bf16(v)
    w_hi, w_lo = _split_bf16(wr_ref[...])
    logits = (jnp.dot(v_hi, w_hi, preferred_element_type=F32)
              + jnp.dot(v_lo, w_hi, preferred_element_type=F32)
              + jnp.dot(v_hi, w_lo, preferred_element_type=F32)) + br_ref[...]
    n_exp = logits.shape[1]
    col = lax.broadcasted_iota(jnp.int32, logits.shape, 1).astype(F32)
    work = logits
    vals, idxs = [], []
    for _ in range(TOP_K):
        m = jnp.max(work, axis=-1, keepdims=True)
        idx = jnp.min(jnp.where(work == m, col, float(n_exp)), axis=-1, keepdims=True)
        vals.append(m)
        idxs.append(idx)
        work = jnp.where(col == idx, -jnp.inf, work)
    exps = [jnp.exp(x - vals[0]) for x in vals]
    denom = exps[0]
    for e in exps[1:]:
        denom = denom + e
    tw = jnp.zeros(logits.shape, F32)
    ti = jnp.zeros(logits.shape, F32)
    selected = jnp.zeros(logits.shape, F32)
    for k in range(TOP_K):
        tw = jnp.where(col == float(k), exps[k] / denom, tw)
        ti = jnp.where(col == float(k), idxs[k], ti)
        selected = jnp.where(col == idxs[k], 1.0, selected)
    tr = logits.shape[0]
    earlier = (lax.broadcasted_iota(jnp.int32, (tr, tr), 1)
               < lax.broadcasted_iota(jnp.int32, (tr, tr), 0)).astype(BF16)
    before = jnp.dot(earlier, selected.astype(BF16), preferred_element_type=F32)
    rk = jnp.zeros(logits.shape, F32)
    for k in range(TOP_K):
        rank_k = jnp.sum(jnp.where(col == idxs[k], before, 0.0), axis=-1, keepdims=True)
        rk = jnp.where(col == float(k), rank_k, rk)
    tw_ref[0] = tw
    ti_ref[0] = ti.astype(jnp.int32)
    rk_ref[0] = rk.astype(jnp.int32)
    cnt_ref[...] = jnp.sum(selected, axis=0, keepdims=True).astype(jnp.int32)


ROUTER_TILE = 256


def _router(h, g, ml, row_of_batch, w_r, b_r):
    b, l, d = h.shape
    n_exp = w_r.shape[1]
    tr = _tile(l, ROUTER_TILE)
    nt = l // tr
    blk = lambda w: pl.BlockSpec((1, tr, w), lambda i, j: (i, j, 0))
    return pl.pallas_call(
        _router_kernel,
        grid=(b, nt),
        in_specs=[blk(d),
                  pl.BlockSpec((1, d), lambda i, j: (0, 0)),
                  _mod_spec(3, d, lambda i, j: row_of_batch(i), lambda i, j: 0),
                  _mod_spec(4, d, lambda i, j: row_of_batch(i), lambda i, j: 0),
                  pl.BlockSpec((d, n_exp), lambda i, j: (0, 0)),
                  pl.BlockSpec((1, n_exp), lambda i, j: (0, 0))],
        out_specs=[blk(d // 2), blk(n_exp), blk(n_exp), blk(n_exp),
                   pl.BlockSpec((None, None, 1, n_exp), lambda i, j: (i, j, 0, 0))],
        out_shape=[jax.ShapeDtypeStruct((b, l, d // 2), jnp.uint32),
                   jax.ShapeDtypeStruct((b, l, n_exp), F32),
                   jax.ShapeDtypeStruct((b, l, n_exp), jnp.int32),
                   jax.ShapeDtypeStruct((b, l, n_exp), jnp.int32),
                   jax.ShapeDtypeStruct((b, nt, 1, n_exp), jnp.int32)],
        compiler_params=_params(),
        name="router",
    )(h, g.reshape(1, d), ml, ml, w_r, b_r.reshape(1, n_exp))


EXPERT_TILE = 256
DISPATCH_TILE = 256


def _route_tables(top_i, rank, tile_counts, n_exp):
    n_pairs = top_i.size
    counts = jnp.sum(tile_counts, axis=0)
    padded = (counts + EXPERT_TILE - 1) // EXPERT_TILE * EXPERT_TILE
    ends = jnp.cumsum(padded)
    starts = ends - padded
    tile_base = starts[None, :] + jnp.cumsum(tile_counts, axis=0) - tile_counts
    experts = jnp.arange(n_exp, dtype=jnp.int32)
    base = jnp.sum(jnp.where(top_i[..., None] == experts, tile_base[:, None, None, :], 0), axis=-1)
    pos = (base + rank).reshape(-1).astype(jnp.int32)
    n_tiles = (n_pairs + n_exp * (EXPERT_TILE - 1)) // EXPERT_TILE + 1
    tile_start = jnp.arange(n_tiles, dtype=jnp.int32) * EXPERT_TILE
    tile_e = jnp.minimum(jnp.searchsorted(ends, tile_start, side="right"), n_exp - 1).astype(jnp.int32)
    tile_valid = (tile_start < ends[-1]).astype(jnp.int32)
    pad = jnp.stack([starts + counts, ends]).astype(jnp.int32)
    return pos, tile_e, tile_valid, n_tiles, pad


def _row_copy(src, src_row, dst, dst_row, sem):
    return pltpu.make_async_copy(src.at[pl.ds(src_row, 1), :], dst.at[pl.ds(dst_row, 1), :], sem)


def _dispatch_kernel(pos_ref, pad_ref, v_ref, xs_ref, zeros, sem, zsem):
    i = pl.program_id(0)
    tt = v_ref.shape[0]
    base = i * (tt * TOP_K)

    def issue(r, carry):
        for k in range(TOP_K):
            _row_copy(v_ref, r, xs_ref, pos_ref[base + r * TOP_K + k], sem).start()
        return carry

    lax.fori_loop(0, tt, issue, 0)

    @pl.when(i == 0)
    def _():
        zeros[...] = jnp.zeros(zeros.shape, zeros.dtype)
        n_exp = pad_ref.shape[1]
        zt = zeros.shape[0]
        n_tiles = xs_ref.shape[0] // zt
        first_free = pad_ref[1, n_exp - 1] // zt

        def tile_copy(t):
            return pltpu.make_async_copy(zeros, xs_ref.at[pl.ds(pl.multiple_of(t * zt, zt), zt), :], zsem)

        def for_each_gap(row_fn, tile_fn):
            def per_expert(e, carry):
                lax.fori_loop(pad_ref[0, e], pad_ref[1, e], lambda r, c: (row_fn(r), c)[1], 0)
                return carry
            lax.fori_loop(0, n_exp, per_expert, 0)
            lax.fori_loop(first_free, n_tiles, lambda t, c: (tile_fn(t), c)[1], 0)

        for_each_gap(lambda r: _row_copy(zeros, 0, xs_ref, r, zsem).start(), lambda t: tile_copy(t).start())
        for_each_gap(lambda r: _row_copy(zeros, 0, xs_ref, r, zsem).wait(), lambda t: tile_copy(t).wait())

    for k in range(TOP_K):
        pltpu.make_async_copy(v_ref, xs_ref.at[pl.ds(0, tt), :], sem).wait()


def _dispatch(v, pos, pad, n_rows):
    t, d = v.shape
    tt = _tile(t, DISPATCH_TILE)
    return pl.pallas_call(
        _dispatch_kernel,
        grid_spec=pltpu.PrefetchScalarGridSpec(
            num_scalar_prefetch=2,
            grid=(t // tt,),
            in_specs=[pl.BlockSpec((tt, d), lambda i, pos_ref, pad_ref: (i, 0))],
            out_specs=pl.BlockSpec(memory_space=pl.ANY),
            scratch_shapes=[pltpu.VMEM((EXPERT_TILE, d), v.dtype),
                            pltpu.SemaphoreType.DMA(()), pltpu.SemaphoreType.DMA(())]),
        out_shape=jax.ShapeDtypeStruct((n_rows, d), v.dtype),
        compiler_params=_params(),
        name="moe_dispatch",
    )(pos, pad, v)


def _experts_kernel(te_ref, tv_ref, x_ref, wgu_ref, bgu_ref, wdn_ref, bdn_ref, y_ref, wgu_bf, wdn_bf):
    i = pl.program_id(0)
    half = x_ref.shape[1]

    @pl.when(jnp.logical_or(i == 0, te_ref[i] != te_ref[jnp.maximum(i - 1, 0)]))
    def _():
        wgu_bf[...] = wgu_ref[0].astype(BF16)
        wdn_bf[...] = wdn_ref[0].astype(BF16)

    @pl.when(tv_ref[i] > 0)
    def _():
        de = wdn_ref.shape[1]
        lo, hi = _unpack_pair(x_ref[...])
        hgu = (jnp.dot(lo.astype(BF16), wgu_bf[:half, :], preferred_element_type=F32)
               + jnp.dot(hi.astype(BF16), wgu_bf[half:, :], preferred_element_type=F32) + bgu_ref[0])
        g_lin = jnp.minimum(hgu[:, :de], SWIGLU_LIMIT)
        up = jnp.clip(hgu[:, de:], -SWIGLU_LIMIT, SWIGLU_LIMIT)
        act = (g_lin * jax.nn.sigmoid(SWIGLU_ALPHA * g_lin) * (up + 1.0)).astype(BF16)
        y_lo = jnp.dot(act, wdn_bf[:, :half], preferred_element_type=F32) + bdn_ref[0, :, :half]
        y_hi = jnp.dot(act, wdn_bf[:, half:], preferred_element_type=F32) + bdn_ref[0, :, half:]
        y_ref[...] = _pack_pair(y_lo, y_hi)

    @pl.when(tv_ref[i] == 0)
    def _():
        y_ref[...] = jnp.zeros(y_ref.shape, y_ref.dtype)


def _experts(xs, tile_e, tile_valid, w_gu, b_gu, w_dn, b_dn):
    n_rows, half = xs.shape
    n_exp, d, two_de = w_gu.shape
    de = two_de // 2
    n_tiles = n_rows // EXPERT_TILE
    return pl.pallas_call(
        _experts_kernel,
        grid_spec=pltpu.PrefetchScalarGridSpec(
            num_scalar_prefetch=2,
            grid=(n_tiles,),
            in_specs=[pl.BlockSpec((EXPERT_TILE, half), lambda i, te, tv: (i, 0)),
                      pl.BlockSpec((1, d, two_de), lambda i, te, tv: (te[i], 0, 0)),
                      pl.BlockSpec((1, 1, two_de), lambda i, te, tv: (te[i], 0, 0)),
                      pl.BlockSpec((1, de, d), lambda i, te, tv: (te[i], 0, 0)),
                      pl.BlockSpec((1, 1, d), lambda i, te, tv: (te[i], 0, 0))],
            out_specs=pl.BlockSpec((EXPERT_TILE, half), lambda i, te, tv: (i, 0)),
            scratch_shapes=[pltpu.VMEM((d, two_de), BF16), pltpu.VMEM((de, d), BF16)]),
        out_shape=jax.ShapeDtypeStruct((n_rows, half), xs.dtype),
        compiler_params=_params(),
        name="moe_experts",
    )(tile_e, tile_valid, xs, w_gu, b_gu.reshape(n_exp, 1, two_de), w_dn, b_dn.reshape(n_exp, 1, d))


COMBINE_TILE = 256
COMBINE_LANES = 512


def _combine_kernel(pos_ref, ys_ref, tw_ref, gate_ref, h_ref, o_ref, buf, sem):
    i = pl.program_id(0) * pl.num_programs(1) + pl.program_id(1)
    tc = h_ref.shape[1]
    base = i * (tc * TOP_K)

    def issue(r, carry):
        for k in range(TOP_K):
            _row_copy(ys_ref, pos_ref[base + r * TOP_K + k], buf.at[k], r, sem).start()
        return carry

    lax.fori_loop(0, tc, issue, 0)
    for k in range(TOP_K):
        pltpu.make_async_copy(ys_ref.at[pl.ds(0, tc), :], buf.at[k], sem).wait()
    half = buf.shape[2]

    def rows(r, carry):
        r0 = pl.multiple_of(r * SUBLANE, SUBLANE)
        tw = tw_ref[0, pl.ds(r0, SUBLANE), :]
        wks = [jnp.broadcast_to(tw[:, k:k + 1], (SUBLANE, COMBINE_LANES)) for k in range(TOP_K)]
        for c in range(half // COMBINE_LANES):
            lanes = slice(c * COMBINE_LANES, (c + 1) * COMBINE_LANES)
            f_lo, f_hi = None, None
            for k in range(TOP_K):
                lo, hi = _unpack_pair(buf[k, pl.ds(r0, SUBLANE), lanes])
                f_lo = lo * wks[k] if f_lo is None else f_lo + lo * wks[k]
                f_hi = hi * wks[k] if f_hi is None else f_hi + hi * wks[k]
            hi_lanes = slice(half + c * COMBINE_LANES, half + (c + 1) * COMBINE_LANES)
            o_ref[0, pl.ds(r0, SUBLANE), lanes] = (h_ref[0, pl.ds(r0, SUBLANE), lanes]
                                                   + gate_ref[:, lanes] * f_lo)
            o_ref[0, pl.ds(r0, SUBLANE), hi_lanes] = (h_ref[0, pl.ds(r0, SUBLANE), hi_lanes]
                                                      + gate_ref[:, hi_lanes] * f_hi)
        return carry

    lax.fori_loop(0, tc // SUBLANE, rows, 0)


def _combine(ys, pos, top_w, h, ml, row_of_batch):
    b, l, d = h.shape
    n_exp = top_w.shape[-1]
    tc = _tile(l, COMBINE_TILE)
    return pl.pallas_call(
        _combine_kernel,
        grid_spec=pltpu.PrefetchScalarGridSpec(
            num_scalar_prefetch=1,
            grid=(b, l // tc),
            in_specs=[pl.BlockSpec(memory_space=pl.ANY),
                      pl.BlockSpec((1, tc, n_exp), lambda bi, i, p: (bi, i, 0)),
                      _mod_spec(5, d, lambda bi, i, p: row_of_batch(bi), lambda bi, i, p: 0),
                      pl.BlockSpec((1, tc, d), lambda bi, i, p: (bi, i, 0))],
            out_specs=pl.BlockSpec((1, tc, d), lambda bi, i, p: (bi, i, 0)),
            scratch_shapes=[pltpu.VMEM((TOP_K, tc, d // 2), ys.dtype), pltpu.SemaphoreType.DMA(())]),
        out_shape=jax.ShapeDtypeStruct((b, l, d), F32),
        input_output_aliases={4: 0},
        compiler_params=_params(),
        name="moe_combine",
    )(pos, ys, top_w, ml, h)


def _moe(h, g, ml, row_of_batch, w_r, b_r, w_gu, b_gu, w_dn, b_dn):
    b, l, d = h.shape
    n_exp = w_r.shape[1]
    v, top_w, top_i, rank, tile_counts = _router(h, g, ml, row_of_batch, w_r, b_r)
    per_tile = lambda a: a[..., :TOP_K].reshape(tile_counts.shape[0] * tile_counts.shape[1], -1, TOP_K)
    pos, tile_e, tile_valid, n_tiles, pad = _route_tables(
        per_tile(top_i), per_tile(rank), tile_counts.reshape(-1, n_exp), n_exp)
    xs = _dispatch(v.reshape(b * l, d // 2), pos, pad, n_tiles * EXPERT_TILE)
    ys = _experts(xs, tile_e, tile_valid, w_gu, b_gu, w_dn, b_dn)
    return _combine(ys, pos, top_w, h, ml, row_of_batch)


def kernel(x, c, ctx, c_ctx, norm1_g, norm2_g, final_g, mod_a, mod_b, mod_bias, w_o, b_o, diff_w_in, diff_lambda, diff_subln_g, gqa_w_in, gqa_q_norm, gqa_k_norm, conv_pw1, conv_pw1_b, conv_dw, conv_dw_b, conv_ln_g, conv_ln_b, router_w, router_b, exp_w_gu, exp_b_gu, exp_w_dn, exp_b_dn):
    b, s, d = x.shape
    lc = ctx.shape[1]
    depth = mod_a.shape[0]
    rows = s // GRID_W
    cos, sin = _rope_tables(rows)
    ctx_last = max([i for i in range(depth) if i % N_MIXERS in ATTN_KINDS], default=-1)

    n_rows = -(-(b + 1) // SUBLANE) * SUBLANE
    cond = jnp.zeros((n_rows, d), F32).at[:b].set(c).at[b].set(c_ctx)
    m_all = _adaln(cond, mod_a, mod_b, mod_bias).reshape(depth, n_rows, 6, 1, d)
    lat_row = lambda bi: bi
    ctx_row = lambda bi: b

    h_lat, h_ctx = x, ctx
    for i in range(depth):
        kind = i % N_MIXERS
        j = i // N_MIXERS
        upd_ctx = i < ctx_last
        need_ctx = upd_ctx or kind in ATTN_KINDS
        ml = m_all[i]
        w_o_i = w_o[i].astype(BF16)

        u_lat = _norm_mod(h_lat, norm1_g[i], ml, 0, lat_row)
        u_ctx = _norm_mod(h_ctx, norm1_g[i], ml, 0, ctx_row) if need_ctx else None
        o_ctx = None
        if kind == 0:
            o_lat = _fourier_mix(u_lat)
            if upd_ctx:
                o_ctx = _fourier_mix(u_ctx)
        elif kind == 1:
            lam_init = 0.8 - 0.6 * math.exp(-0.3 * i)
            w_in = diff_w_in[j].astype(BF16)
            qkv_lat = _qkv(u_lat.reshape(b * s, d), w_in, cos, sin, d, d, True).reshape(b, s, 3 * d)
            qkv_ctx = _qkv(u_ctx.reshape(b * lc, d), w_in, cos, sin, d, d, False).reshape(b, lc, 3 * d)
            o_lat = _attn_diff(qkv_lat, [qkv_ctx, qkv_lat], diff_lambda[j], diff_subln_g[j], lam_init, d)
            if upd_ctx:
                o_ctx = _attn_diff(qkv_ctx, [qkv_ctx], diff_lambda[j], diff_subln_g[j], lam_init, d)
        elif kind == 2:
            n_heads = d // HEAD_DIM
            nq, nk = n_heads * HEAD_DIM, n_heads // GQA_GROUP * HEAD_DIM
            w_in = gqa_w_in[j].astype(BF16)
            gains = (gqa_q_norm[j], gqa_k_norm[j])
            qkv_lat = _qkv(u_lat.reshape(b * s, d), w_in, cos, sin, nq, nk, True, gains).reshape(b, s, -1)
            qkv_ctx = _qkv(u_ctx.reshape(b * lc, d), w_in, cos, sin, nq, nk, False, gains).reshape(b, lc, -1)
            o_lat = _attn_gqa(qkv_lat, [qkv_ctx, qkv_lat], n_heads)
            if upd_ctx:
                o_ctx = _attn_gqa(qkv_ctx, [qkv_ctx], n_heads)
        else:
            pw1 = conv_pw1[j].astype(BF16)

            def conv(u):
                bb, ll, _ = u.shape
                hg = _linear_glu(u.reshape(bb * ll, d), pw1, conv_pw1_b[j]).reshape(bb, ll, d)
                return _conv_ln_silu(hg, conv_dw[j], conv_dw_b[j], conv_ln_g[j], conv_ln_b[j])

            o_lat = conv(u_lat)
            if upd_ctx:
                o_ctx = conv(u_ctx)

        moe_w = (router_w[i], router_b[i], exp_w_gu[i], exp_b_gu[i], exp_w_dn[i], exp_b_dn[i])
        h_lat = _linear_residual(o_lat, w_o_i, b_o[i], h_lat, ml, 2, lat_row)
        if upd_ctx:
            h_ctx = _linear_residual(o_ctx, w_o_i, b_o[i], h_ctx, ml, 2, ctx_row)
            h_ctx = _moe(h_ctx, norm2_g[i], ml, ctx_row, *moe_w)
        h_lat = _moe(h_lat, norm2_g[i], ml, lat_row, *moe_w)

    return _final_norm(h_lat, final_g)
```

```python
import functools
import math

import jax
import jax.numpy as jnp
from jax import lax
from jax.experimental import pallas as pl
from jax.experimental.pallas import tpu as pltpu

F32 = jnp.float32
BF16 = jnp.bfloat16

EPS = 1e-6
HEAD_DIM = 128
ROPE_THETA = 10000.0
GRID_W = 64
FOURIER_GROUPS = 8
GQA_GROUP = 4
TOP_K = 4
SWIGLU_LIMIT = 7.0
SWIGLU_ALPHA = 1.702
N_MIXERS = 4
ATTN_KINDS = (1, 2)

V7X_VMEM_BYTES = 64 * 1024 * 1024
VMEM_LIMIT = V7X_VMEM_BYTES - 8 * 1024 * 1024
LANE = 128
SUBLANE = 8


def _params(*sem):
    return pltpu.CompilerParams(dimension_semantics=sem if sem else None,
                                vmem_limit_bytes=VMEM_LIMIT)


def _tile(n, pref):
    t = min(n, pref)
    assert n % t == 0, (n, pref)
    return t


def _adaln_kernel(c_ref, a_ref, b_ref, bias_ref, o_ref):
    c = c_ref[...]
    s = c * jax.nn.sigmoid(c)
    t = jnp.dot(s, a_ref[0], preferred_element_type=F32)
    o_ref[0] = jnp.dot(t, b_ref[0], preferred_element_type=F32) + bias_ref[0]


def _adaln(cond, mod_a, mod_b, mod_bias):
    depth, d, rank = mod_a.shape
    n = mod_b.shape[-1]
    r = cond.shape[0]
    tn = _tile(n, 2048)
    return pl.pallas_call(
        _adaln_kernel,
        grid=(depth, n // tn),
        in_specs=[pl.BlockSpec((r, d), lambda i, j: (0, 0)),
                  pl.BlockSpec((1, d, rank), lambda i, j: (i, 0, 0)),
                  pl.BlockSpec((1, rank, tn), lambda i, j: (i, 0, j)),
                  pl.BlockSpec((1, 1, tn), lambda i, j: (i, 0, j))],
        out_specs=pl.BlockSpec((1, r, tn), lambda i, j: (i, 0, j)),
        out_shape=jax.ShapeDtypeStruct((depth, r, n), F32),
        compiler_params=_params(),
        name="adaln",
    )(cond, mod_a, mod_b, mod_bias.reshape(depth, 1, n))


def _mod_spec(chunk, width, row_fn, col_fn):
    return pl.BlockSpec((None, None, 1, width),
                        lambda *idx: (row_fn(*idx), chunk, 0, col_fn(*idx)))


def _rms(x):
    return x * lax.rsqrt(jnp.mean(x * x, axis=-1, keepdims=True) + EPS)


def _norm_mod_kernel(h_ref, g_ref, sh_ref, sc_ref, o_ref):
    y = _rms(h_ref[0]) * g_ref[...]
    o_ref[0] = (y * (1.0 + sc_ref[...]) + sh_ref[...]).astype(o_ref.dtype)


def _norm_mod(h, g, ml, chunk0, row_of_batch):
    b, l, d = h.shape
    tr = _tile(l, 256)
    return pl.pallas_call(
        _norm_mod_kernel,
        grid=(b, l // tr),
        in_specs=[pl.BlockSpec((1, tr, d), lambda i, j: (i, j, 0)),
                  pl.BlockSpec((1, d), lambda i, j: (0, 0)),
                  _mod_spec(chunk0, d, lambda i, j: row_of_batch(i), lambda i, j: 0),
                  _mod_spec(chunk0 + 1, d, lambda i, j: row_of_batch(i), lambda i, j: 0)],
        out_specs=pl.BlockSpec((1, tr, d), lambda i, j: (i, j, 0)),
        out_shape=jax.ShapeDtypeStruct((b, l, d), BF16),
        compiler_params=_params(),
        name="norm_mod",
    )(h, g.reshape(1, d), ml, ml)


def _linear_kernel(a_ref, w_ref, o_ref):
    o_ref[0] = jnp.dot(a_ref[...], w_ref[0], preferred_element_type=F32).astype(o_ref.dtype)


def _linear(a, w, out_dtype, tm=512, tn=512):
    m, k = a.shape
    g, _, n = w.shape
    tm, tn = _tile(m, tm), _tile(n, tn)
    return pl.pallas_call(
        _linear_kernel,
        grid=(g, m // tm, n // tn),
        in_specs=[pl.BlockSpec((tm, k), lambda b, i, j: (i, 0)),
                  pl.BlockSpec((1, k, tn), lambda b, i, j: (b, 0, j))],
        out_specs=pl.BlockSpec((1, tm, tn), lambda b, i, j: (b, i, j)),
        out_shape=jax.ShapeDtypeStruct((g, m, n), out_dtype),
        compiler_params=_params(),
        name="linear",
    )(a, w)


def _linear_residual_kernel(a_ref, w_ref, b_ref, gate_ref, h_ref, o_ref):
    acc = jnp.dot(a_ref[0], w_ref[...], preferred_element_type=F32) + b_ref[...]
    o_ref[0] = h_ref[0] + gate_ref[...] * acc


def _linear_residual(o, w, bias, h, ml, gate_chunk, row_of_batch):
    b, l, k = o.shape
    n = w.shape[1]
    tm, tn = _tile(l, 1024), _tile(n, 512)
    return pl.pallas_call(
        _linear_residual_kernel,
        grid=(b, l // tm, n // tn),
        in_specs=[pl.BlockSpec((1, tm, k), lambda bi, i, j: (bi, i, 0)),
                  pl.BlockSpec((k, tn), lambda bi, i, j: (0, j)),
                  pl.BlockSpec((1, tn), lambda bi, i, j: (0, j)),
                  _mod_spec(gate_chunk, tn, lambda bi, i, j: row_of_batch(bi), lambda bi, i, j: j),
                  pl.BlockSpec((1, tm, tn), lambda bi, i, j: (bi, i, j))],
        out_specs=pl.BlockSpec((1, tm, tn), lambda bi, i, j: (bi, i, j)),
        out_shape=jax.ShapeDtypeStruct((b, l, n), F32),
        input_output_aliases={4: 0},
        compiler_params=_params(),
        name="linear_residual",
    )(o, w, bias.reshape(1, n), ml, h)


def _linear_glu_kernel(a_ref, w1_ref, w2_ref, b1_ref, b2_ref, o_ref):
    a = a_ref[...]
    lin = jnp.dot(a, w1_ref[...], preferred_element_type=F32) + b1_ref[...]
    gate = jnp.dot(a, w2_ref[...], preferred_element_type=F32) + b2_ref[...]
    o_ref[...] = (lin * jax.nn.sigmoid(gate)).astype(o_ref.dtype)


def _linear_glu(a, w, bias):
    m, k = a.shape
    n = w.shape[1] // 2
    tm, tn = _tile(m, 1024), _tile(n, 256)
    nj = n // tn
    bias = bias.reshape(1, 2 * n)
    return pl.pallas_call(
        _linear_glu_kernel,
        grid=(m // tm, nj),
        in_specs=[pl.BlockSpec((tm, k), lambda i, j: (i, 0)),
                  pl.BlockSpec((k, tn), lambda i, j: (0, j)),
                  pl.BlockSpec((k, tn), lambda i, j: (0, j + nj)),
                  pl.BlockSpec((1, tn), lambda i, j: (0, j)),
                  pl.BlockSpec((1, tn), lambda i, j: (0, j + nj))],
        out_specs=pl.BlockSpec((tm, tn), lambda i, j: (i, j)),
        out_shape=jax.ShapeDtypeStruct((m, n), BF16),
        compiler_params=_params(),
        name="linear_glu",
    )(a, w, w, bias, bias)


def _rope_tables(rows):
    nf = HEAD_DIM // 4
    row = jnp.repeat(jnp.arange(rows, dtype=F32), GRID_W)
    col = jnp.tile(jnp.arange(GRID_W, dtype=F32), rows)
    inv = ROPE_THETA ** (-jnp.arange(nf, dtype=F32) / nf)
    ar, ac = row[:, None] * inv, col[:, None] * inv
    cos = jnp.concatenate([jnp.cos(ar), jnp.cos(ar), jnp.cos(ac), jnp.cos(ac)], axis=1)
    sin = jnp.concatenate([-jnp.sin(ar), jnp.sin(ar), -jnp.sin(ac), jnp.sin(ac)], axis=1)
    return cos, sin


def _rope(x, cos, sin):
    nf = HEAD_DIM // 4
    lane = lax.broadcasted_iota(jnp.int32, x.shape, 1)
    first = (lane % (2 * nf)) < nf
    partner = jnp.where(first, pltpu.roll(x, HEAD_DIM - nf, 1), pltpu.roll(x, nf, 1))
    return x * cos + partner * sin


def _qkv_diff_kernel(a_ref, w_ref, cos_ref, sin_ref, o_ref, *, n_q, n_k, rope, scale):
    j = pl.program_id(1)
    acc = jnp.dot(a_ref[...], w_ref[...], preferred_element_type=F32)
    tn = acc.shape[1]

    def store(mult):
        for c in range(tn // HEAD_DIM):
            x = acc[:, c * HEAD_DIM:(c + 1) * HEAD_DIM]
            if rope:
                x = _rope(x, cos_ref[...], sin_ref[...])
            o_ref[:, c * HEAD_DIM:(c + 1) * HEAD_DIM] = (x * mult).astype(o_ref.dtype)

    @pl.when(j < n_q)
    def _():
        store(scale)

    @pl.when(jnp.logical_and(j >= n_q, j < n_q + n_k))
    def _():
        store(1.0)

    @pl.when(j >= n_q + n_k)
    def _():
        o_ref[...] = acc.astype(o_ref.dtype)


def _qkv_gqa_kernel(a_ref, w_ref, cos_ref, sin_ref, qg_ref, kg_ref, o_ref, *, n_q, n_k, rope, scale):
    j = pl.program_id(1)
    acc = jnp.dot(a_ref[...], w_ref[...], preferred_element_type=F32)
    tn = acc.shape[1]

    def store(g_ref, mult):
        for c in range(tn // HEAD_DIM):
            x = _rms(acc[:, c * HEAD_DIM:(c + 1) * HEAD_DIM]) * g_ref[...]
            if rope:
                x = _rope(x, cos_ref[...], sin_ref[...])
            o_ref[:, c * HEAD_DIM:(c + 1) * HEAD_DIM] = (x * mult).astype(o_ref.dtype)

    @pl.when(j < n_q)
    def _():
        store(qg_ref, scale)

    @pl.when(jnp.logical_and(j >= n_q, j < n_q + n_k))
    def _():
        store(kg_ref, 1.0)

    @pl.when(j >= n_q + n_k)
    def _():
        o_ref[...] = acc.astype(o_ref.dtype)


def _qkv(a, w, cos, sin, n_q_cols, n_k_cols, rope, gains=None):
    m, k = a.shape
    n = w.shape[1]
    s = cos.shape[0]
    tm, tn = _tile(min(m, s), 1024), _tile(n, 512)
    assert n_q_cols % tn == 0 and n_k_cols % tn == 0
    npos = s // tm
    kw = dict(n_q=n_q_cols // tn, n_k=n_k_cols // tn, rope=rope, scale=HEAD_DIM ** -0.5 * math.log2(math.e))
    in_specs = [pl.BlockSpec((tm, k), lambda i, j: (i, 0)),
                pl.BlockSpec((k, tn), lambda i, j: (0, j)),
                pl.BlockSpec((tm, HEAD_DIM), lambda i, j: (i % npos, 0)),
                pl.BlockSpec((tm, HEAD_DIM), lambda i, j: (i % npos, 0))]
    args = [a, w, cos, sin]
    if gains is None:
        kernel = functools.partial(_qkv_diff_kernel, **kw)
    else:
        kernel = functools.partial(_qkv_gqa_kernel, **kw)
        in_specs += [pl.BlockSpec((1, HEAD_DIM), lambda i, j: (0, 0))] * 2
        args += [gains[0].reshape(1, HEAD_DIM), gains[1].reshape(1, HEAD_DIM)]
    return pl.pallas_call(
        kernel,
        grid=(m // tm, n // tn),
        in_specs=in_specs,
        out_specs=pl.BlockSpec((tm, tn), lambda i, j: (i, j)),
        out_shape=jax.ShapeDtypeStruct((m, n), BF16),
        compiler_params=_params(),
        name="qkv",
    )(*args)


def _dft_cos_sin(n):
    k = jnp.arange(n, dtype=jnp.int32)
    ang = ((k[:, None] * k[None, :]) % n).astype(F32) * (2.0 * math.pi / n)
    s = 1.0 / math.sqrt(n)
    return jnp.cos(ang) * s, jnp.sin(ang) * s


def _fourier_channels_kernel(u_ref, w_ref, o_ref, *, groups):
    cg = w_ref.shape[0]
    for g in range(groups):
        x = u_ref[0, :, g * cg:(g + 1) * cg]
        y = jnp.dot(x, w_ref[...], preferred_element_type=F32)
        o_ref[0, 0, :, g * cg:(g + 1) * cg] = y[:, :cg].astype(o_ref.dtype)
        o_ref[0, 1, :, g * cg:(g + 1) * cg] = y[:, cg:].astype(o_ref.dtype)


def _fourier_mix(u):
    b, l, d = u.shape
    cg = d // FOURIER_GROUPS
    cc, sc = _dft_cos_sin(cg)
    wch = jnp.concatenate([cc, -sc], axis=1).astype(BF16)
    tm = _tile(l, 512)
    y = pl.pallas_call(
        functools.partial(_fourier_channels_kernel, groups=FOURIER_GROUPS),
        grid=(b, l // tm),
        in_specs=[pl.BlockSpec((1, tm, d), lambda i, j: (i, j, 0)),
                  pl.BlockSpec((cg, 2 * cg), lambda i, j: (0, 0))],
        out_specs=pl.BlockSpec((1, 2, tm, d), lambda i, j: (i, 0, j, 0)),
        out_shape=jax.ShapeDtypeStruct((b, 2, l, d), BF16),
        compiler_params=_params(),
        name="fourier_channels",
    )(u, wch)
    cl, sl = _dft_cos_sin(l)
    wpos = jnp.concatenate([cl, sl], axis=1).astype(BF16)
    return _linear(wpos, y.reshape(b, 2 * l, d), BF16)


def _softmax_parts(s):
    m = jnp.max(s, axis=-1, keepdims=True)
    p = jnp.exp2(s - m)
    return p, jnp.sum(p, axis=-1, keepdims=True)


def _nt_dot(a, b):
    return lax.dot_general(a, b, (((1,), (1,)), ((), ())), preferred_element_type=F32)


def _attn_diff_kernel(*refs, n_src, tq, lam_init):
    lam_ref, g_ref, q_ref = refs[:3]
    kv_refs = refs[3:3 + 2 * n_src]
    o_ref, kbuf, vbuf = refs[3 + 2 * n_src:]
    off = 0
    for s in range(n_src):
        k_ref, v_ref = kv_refs[2 * s], kv_refs[2 * s + 1]
        n = k_ref.shape[0]
        kbuf[off:off + n, :] = k_ref[...]
        vbuf[off:off + n, :] = v_ref[...]
        off += n
    lp = lam_ref[...]
    lam = (jnp.exp(jnp.sum(lp[0:1] * lp[1:2], axis=-1, keepdims=True))
           - jnp.exp(jnp.sum(lp[2:3] * lp[3:4], axis=-1, keepdims=True)) + lam_init)

    def body(qi, carry):
        r0 = pl.multiple_of(qi * tq, tq)
        q = q_ref[pl.ds(r0, tq), :]
        p0, l0 = _softmax_parts(_nt_dot(q[:, :HEAD_DIM], kbuf[:, :HEAD_DIM]))
        p1, l1 = _softmax_parts(_nt_dot(q[:, HEAD_DIM:], kbuf[:, HEAD_DIM:]))
        a = p0 * (1.0 / l0) - p1 * (lam / l1)
        o = jnp.dot(a.astype(BF16), vbuf[...], preferred_element_type=F32)
        o = _rms(o) * g_ref[...] * (1.0 - lam_init)
        o_ref[pl.ds(r0, tq), :] = o.astype(o_ref.dtype)
        return carry

    n_q = q_ref.shape[0] // tq
    lax.fori_loop(0, n_q, body, 0, unroll=2 if n_q % 2 == 0 else 1)


def _attn_diff(q_src, kv_srcs, lam_p, subln_g, lam_init, d_model):
    b, lq, _ = q_src.shape
    hw = 2 * HEAD_DIM
    nh = d_model // hw
    lk = sum(x.shape[1] for x in kv_srcs)
    tq = _tile(lq, 256)
    in_specs = [pl.BlockSpec((4, HEAD_DIM), lambda bi, h: (0, 0)),
                pl.BlockSpec((1, hw), lambda bi, h: (0, 0)),
                pl.BlockSpec((None, lq, hw), lambda bi, h: (bi, 0, h))]
    args = [lam_p, subln_g.reshape(1, hw), q_src]
    for x in kv_srcs:
        in_specs += [pl.BlockSpec((None, x.shape[1], hw), lambda bi, h: (bi, 0, nh + h)),
                     pl.BlockSpec((None, x.shape[1], hw), lambda bi, h: (bi, 0, 2 * nh + h))]
        args += [x, x]
    return pl.pallas_call(
        functools.partial(_attn_diff_kernel, n_src=len(kv_srcs), tq=tq, lam_init=lam_init),
        grid=(b, nh),
        in_specs=in_specs,
        out_specs=pl.BlockSpec((None, lq, hw), lambda bi, h: (bi, 0, h)),
        out_shape=jax.ShapeDtypeStruct((b, lq, d_model), BF16),
        scratch_shapes=[pltpu.VMEM((lk, hw), BF16), pltpu.VMEM((lk, hw), BF16)],
        compiler_params=_params(),
        name="attn_diff",
    )(*args)


def _attn_gqa_kernel(*refs, n_src, tq, group):
    q_ref = refs[0]
    kv_refs = refs[1:1 + 2 * n_src]
    o_ref, kbuf, vbuf = refs[1 + 2 * n_src:]
    off = 0
    for s in range(n_src):
        k_ref, v_ref = kv_refs[2 * s], kv_refs[2 * s + 1]
        n = k_ref.shape[0]
        kbuf[off:off + n, :] = k_ref[...]
        vbuf[off:off + n, :] = v_ref[...]
        off += n

    def body(qi, carry):
        r0 = pl.multiple_of(qi * tq, tq)
        for hh in range(group):
            q = q_ref[pl.ds(r0, tq), hh * HEAD_DIM:(hh + 1) * HEAD_DIM]
            p, l = _softmax_parts(_nt_dot(q, kbuf[...]))
            o = jnp.dot(p.astype(BF16), vbuf[...], preferred_element_type=F32) * (1.0 / l)
            o_ref[pl.ds(r0, tq), hh * HEAD_DIM:(hh + 1) * HEAD_DIM] = o.astype(o_ref.dtype)
        return carry

    lax.fori_loop(0, q_ref.shape[0] // tq, body, 0)


def _attn_gqa(q_src, kv_srcs, n_heads):
    b, lq, _ = q_src.shape
    n_kv = n_heads // GQA_GROUP
    gw = GQA_GROUP * HEAD_DIM
    lk = sum(x.shape[1] for x in kv_srcs)
    tq = _tile(lq, 256)
    in_specs = [pl.BlockSpec((None, lq, gw), lambda bi, g: (bi, 0, g))]
    args = [q_src]
    for x in kv_srcs:
        in_specs += [pl.BlockSpec((None, x.shape[1], HEAD_DIM), lambda bi, g: (bi, 0, n_heads + g)),
                     pl.BlockSpec((None, x.shape[1], HEAD_DIM), lambda bi, g: (bi, 0, n_heads + n_kv + g))]
        args += [x, x]
    return pl.pallas_call(
        functools.partial(_attn_gqa_kernel, n_src=len(kv_srcs), tq=tq, group=GQA_GROUP),
        grid=(b, n_kv),
        in_specs=in_specs,
        out_specs=pl.BlockSpec((None, lq, gw), lambda bi, g: (bi, 0, g)),
        out_shape=jax.ShapeDtypeStruct((b, lq, n_heads * HEAD_DIM), BF16),
        scratch_shapes=[pltpu.VMEM((lk, HEAD_DIM), BF16), pltpu.VMEM((lk, HEAD_DIM), BF16)],
        compiler_params=_params(),
        name="attn_gqa",
    )(*args)


CONV_HALO = 16
CONV_ROWS = 8
CONV_LANES = 256


def _conv_kernel(prev_ref, cur_ref, next_ref, dw_ref, dwb_ref, g_ref, b_ref, o_ref, ext, conv, taps8, *, width):
    i = pl.program_id(1)
    last = pl.num_programs(1) - 1
    ts = cur_ref.shape[0]
    pad = (width - 1) // 2
    ext[0:CONV_HALO, :] = jnp.where(i > 0, prev_ref[...].astype(F32), 0.0)
    ext[CONV_HALO:CONV_HALO + ts, :] = cur_ref[...].astype(F32)
    ext[CONV_HALO + ts:, :] = jnp.where(i < last, next_ref[...].astype(F32), 0.0)

    first = CONV_HALO - pad
    n_blk = (first + width - 1) // CONV_ROWS + 1
    sub = lax.broadcasted_iota(jnp.int32, (CONV_ROWS, CONV_LANES), 0)
    for j in range(width):
        taps8[j] = jnp.broadcast_to(dw_ref[j:j + 1, :], taps8.shape[1:])

    def residue_sums(m, lanes):
        r0 = pl.multiple_of(m * CONV_ROWS, CONV_ROWS)
        blocks = [ext[pl.ds(r0 + CONV_ROWS * a, CONV_ROWS), lanes] for a in range(n_blk)]
        sums = []
        for b in range(CONV_ROWS):
            taps = [(a, CONV_ROWS * a + b - first) for a in range(n_blk)
                    if 0 <= CONV_ROWS * a + b - first < width]
            acc = blocks[taps[0][0]] * taps8[taps[0][1], :, lanes]
            for a, j in taps[1:]:
                acc = acc + blocks[a] * taps8[j, :, lanes]
            sums.append(acc)
        return tuple(sums)

    for c in range(ext.shape[1] // CONV_LANES):
        lanes = slice(c * CONV_LANES, (c + 1) * CONV_LANES)

        def conv_block(m, cur, lanes=lanes):
            nxt = residue_sums(m + 1, lanes)
            out = cur[0] + dwb_ref[:, lanes]
            for b in range(1, CONV_ROWS):
                mixed = jnp.where(sub >= b, cur[b], nxt[b])
                out = out + pltpu.roll(mixed, CONV_ROWS - b, 0)
            conv[pl.ds(pl.multiple_of(m * CONV_ROWS, CONV_ROWS), CONV_ROWS), lanes] = out
            return nxt

        lax.fori_loop(0, ts // CONV_ROWS, conv_block, residue_sums(0, lanes))

    def body(r, carry):
        r0 = pl.multiple_of(r * CONV_ROWS, CONV_ROWS)
        acc = conv[pl.ds(r0, CONV_ROWS), :]
        mu = jnp.mean(acc, axis=-1, keepdims=True)
        xc = acc - mu
        var = jnp.mean(xc * xc, axis=-1, keepdims=True)
        y = xc * lax.rsqrt(var + EPS) * g_ref[...] + b_ref[...]
        o_ref[pl.ds(r0, CONV_ROWS), :] = (y * jax.nn.sigmoid(y)).astype(o_ref.dtype)
        return carry

    lax.fori_loop(0, ts // CONV_ROWS, body, 0)


def _conv_ln_silu(h, dw, dw_b, ln_g, ln_b):
    b, s, d = h.shape
    width = dw.shape[0]
    assert (width - 1) // 2 <= CONV_HALO
    ts = _tile(s, 256)
    nh = ts // CONV_HALO
    n_halo = s // CONV_HALO
    vec = lambda x: x.reshape(1, d)
    vspec = pl.BlockSpec((1, d), lambda bi, i: (0, 0))
    return pl.pallas_call(
        functools.partial(_conv_kernel, width=width),
        grid=(b, s // ts),
        in_specs=[pl.BlockSpec((None, CONV_HALO, d), lambda bi, i: (bi, jnp.maximum(i * nh - 1, 0), 0)),
                  pl.BlockSpec((None, ts, d), lambda bi, i: (bi, i, 0)),
                  pl.BlockSpec((None, CONV_HALO, d), lambda bi, i: (bi, jnp.minimum((i + 1) * nh, n_halo - 1), 0)),
                  pl.BlockSpec((width, d), lambda bi, i: (0, 0)),
                  vspec, vspec, vspec],
        out_specs=pl.BlockSpec((None, ts, d), lambda bi, i: (bi, i, 0)),
        out_shape=jax.ShapeDtypeStruct((b, s, d), BF16),
        scratch_shapes=[pltpu.VMEM((ts + 2 * CONV_HALO, d), F32), pltpu.VMEM((ts, d), F32),
                        pltpu.VMEM((width, CONV_ROWS, d), F32)],
        compiler_params=_params(),
        name="conv_ln_silu",
    )(h, h, h, dw, vec(dw_b), vec(ln_g), vec(ln_b))


def _split_bf16(x):
    hi = x.astype(BF16)
    return hi, (x - hi.astype(F32)).astype(BF16)


def _pack_pair(lo, hi):
    lo_bits = lax.bitcast_convert_type(lo.astype(BF16).astype(F32), jnp.uint32)
    hi_bits = lax.bitcast_convert_type(hi.astype(BF16).astype(F32), jnp.uint32)
    return (lo_bits >> 16) | (hi_bits & jnp.uint32(0xFFFF0000))


def _unpack_pair(w):
    lo = lax.bitcast_convert_type(w << 16, F32)
    hi = lax.bitcast_convert_type(w & jnp.uint32(0xFFFF0000), F32)
    return lo, hi


def _router_kernel(h_ref, g_ref, sh_ref, sc_ref, wr_ref, br_ref, v_ref, tw_ref, ti_ref, rk_ref, cnt_ref):
    v = _rms(h_ref[0]) * g_ref[...] * (1.0 + sc_ref[...]) + sh_ref[...]
    half = v.shape[1] // 2
    v_ref[0] = _pack_pair(v[:, :half], v[:, half:])
    v_hi, v_lo = _split_bf16(v)
    w_hi, w_lo = _split_bf16(wr_ref[...])
    n_exp = w_hi.shape[1]
    both = jnp.dot(v_hi, jnp.concatenate([w_hi, w_lo], axis=1), preferred_element_type=F32)
    logits = (both[:, :n_exp] + both[:, n_exp:]
              + jnp.dot(v_lo, w_hi, preferred_element_type=F32)) + br_ref[...]
    col = lax.broadcasted_iota(jnp.int32, logits.shape, 1).astype(F32)
    work = logits
    vals, idxs = [], []
    for _ in range(TOP_K):
        m = jnp.max(work, axis=-1, keepdims=True)
        idx = jnp.min(jnp.where(work == m, col, float(n_exp)), axis=-1, keepdims=True)
        vals.append(m)
        idxs.append(idx)
        work = jnp.where(col == idx, -jnp.inf, work)
    exps = [jnp.exp(x - vals[0]) for x in vals]
    denom = exps[0]
    for e in exps[1:]:
        denom = denom + e
    tw = jnp.zeros(logits.shape, F32)
    ti = jnp.zeros(logits.shape, F32)
    selected = jnp.zeros(logits.shape, F32)
    for k in range(TOP_K):
        tw = jnp.where(col == float(k), exps[k] / denom, tw)
        ti = jnp.where(col == float(k), idxs[k], ti)
        selected = jnp.where(col == idxs[k], 1.0, selected)
    tr = logits.shape[0]
    earlier = (lax.broadcasted_iota(jnp.int32, (tr, tr), 1)
               < lax.broadcasted_iota(jnp.int32, (tr, tr), 0)).astype(BF16)
    before = jnp.dot(earlier, selected.astype(BF16), preferred_element_type=F32)
    rk = jnp.zeros(logits.shape, F32)
    for k in range(TOP_K):
        rank_k = jnp.sum(jnp.where(col == idxs[k], before, 0.0), axis=-1, keepdims=True)
        rk = jnp.where(col == float(k), rank_k, rk)
    tw_ref[0] = tw
    ti_ref[0] = ti.astype(jnp.int32)
    rk_ref[0] = rk.astype(jnp.int32)
    cnt_ref[...] = jnp.sum(selected, axis=0, keepdims=True).astype(jnp.int32)


ROUTER_TILE = 256


def _router(h, g, ml, row_of_batch, w_r, b_r):
    b, l, d = h.shape
    n_exp = w_r.shape[1]
    tr = _tile(l, ROUTER_TILE)
    nt = l // tr
    blk = lambda w: pl.BlockSpec((1, tr, w), lambda i, j: (i, j, 0))
    return pl.pallas_call(
        _router_kernel,
        grid=(b, nt),
        in_specs=[blk(d),
                  pl.BlockSpec((1, d), lambda i, j: (0, 0)),
                  _mod_spec(3, d, lambda i, j: row_of_batch(i), lambda i, j: 0),
                  _mod_spec(4, d, lambda i, j: row_of_batch(i), lambda i, j: 0),
                  pl.BlockSpec((d, n_exp), lambda i, j: (0, 0)),
                  pl.BlockSpec((1, n_exp), lambda i, j: (0, 0))],
        out_specs=[blk(d // 2), blk(n_exp), blk(n_exp), blk(n_exp),
                   pl.BlockSpec((None, None, 1, n_exp), lambda i, j: (i, j, 0, 0))],
        out_shape=[jax.ShapeDtypeStruct((b, l, d // 2), jnp.uint32),
                   jax.ShapeDtypeStruct((b, l, n_exp), F32),
                   jax.ShapeDtypeStruct((b, l, n_exp), jnp.int32),
                   jax.ShapeDtypeStruct((b, l, n_exp), jnp.int32),
                   jax.ShapeDtypeStruct((b, nt, 1, n_exp), jnp.int32)],
        compiler_params=_params(),
        name="router",
    )(h, g.reshape(1, d), ml, ml, w_r, b_r.reshape(1, n_exp))


EXPERT_TILE = 256
DISPATCH_TILE = 256


def _route_tables(top_i, rank, tile_counts, n_exp):
    n_pairs = top_i.size
    counts = jnp.sum(tile_counts, axis=0)
    padded = (counts + EXPERT_TILE - 1) // EXPERT_TILE * EXPERT_TILE
    ends = jnp.cumsum(padded)
    starts = ends - padded
    tile_base = starts[None, :] + jnp.cumsum(tile_counts, axis=0) - tile_counts
    experts = jnp.arange(n_exp, dtype=jnp.int32)
    base = jnp.sum(jnp.where(top_i[..., None] == experts, tile_base[:, None, None, :], 0), axis=-1)
    pos = (base + rank).reshape(-1).astype(jnp.int32)
    n_tiles = (n_pairs + n_exp * (EXPERT_TILE - 1)) // EXPERT_TILE + 1
    tile_start = jnp.arange(n_tiles, dtype=jnp.int32) * EXPERT_TILE
    tile_e = jnp.sum((ends[None, :] <= tile_start[:, None]).astype(jnp.int32), axis=1)
    tile_e = jnp.minimum(tile_e, n_exp - 1)
    tile_valid = (tile_start < ends[-1]).astype(jnp.int32)
    pad = jnp.stack([starts + counts, ends]).astype(jnp.int32)
    return pos, tile_e, tile_valid, n_tiles, pad


def _row_copy(src, src_row, dst, dst_row, sem):
    return pltpu.make_async_copy(src.at[pl.ds(src_row, 1), :], dst.at[pl.ds(dst_row, 1), :], sem)


def _dispatch_kernel(pos_ref, pad_ref, v_ref, xs_ref, zeros, sem, zsem):
    i = pl.program_id(0)
    tt = v_ref.shape[0]
    base = i * (tt * TOP_K)

    def issue(r, carry):
        for k in range(TOP_K):
            _row_copy(v_ref, r, xs_ref, pos_ref[base + r * TOP_K + k], sem).start()
        return carry

    lax.fori_loop(0, tt, issue, 0)

    @pl.when(i == 0)
    def _():
        zeros[...] = jnp.zeros(zeros.shape, zeros.dtype)
        n_exp = pad_ref.shape[1]
        zt = zeros.shape[0]
        n_tiles = xs_ref.shape[0] // zt
        first_free = pad_ref[1, n_exp - 1] // zt

        def tile_copy(t):
            return pltpu.make_async_copy(zeros, xs_ref.at[pl.ds(pl.multiple_of(t * zt, zt), zt), :], zsem)

        def for_each_gap(row_fn, tile_fn):
            def per_expert(e, carry):
                lax.fori_loop(pad_ref[0, e], pad_ref[1, e], lambda r, c: (row_fn(r), c)[1], 0)
                return carry
            lax.fori_loop(0, n_exp, per_expert, 0)
            lax.fori_loop(first_free, n_tiles, lambda t, c: (tile_fn(t), c)[1], 0)

        for_each_gap(lambda r: _row_copy(zeros, 0, xs_ref, r, zsem).start(), lambda t: tile_copy(t).start())
        for_each_gap(lambda r: _row_copy(zeros, 0, xs_ref, r, zsem).wait(), lambda t: tile_copy(t).wait())

    for k in range(TOP_K):
        pltpu.make_async_copy(v_ref, xs_ref.at[pl.ds(0, tt), :], sem).wait()


def _dispatch(v, pos, pad, n_rows):
    t, d = v.shape
    tt = _tile(t, DISPATCH_TILE)
    return pl.pallas_call(
        _dispatch_kernel,
        grid_spec=pltpu.PrefetchScalarGridSpec(
            num_scalar_prefetch=2,
            grid=(t // tt,),
            in_specs=[pl.BlockSpec((tt, d), lambda i, pos_ref, pad_ref: (i, 0))],
            out_specs=pl.BlockSpec(memory_space=pl.ANY),
            scratch_shapes=[pltpu.VMEM((EXPERT_TILE, d), v.dtype),
                            pltpu.SemaphoreType.DMA(()), pltpu.SemaphoreType.DMA(())]),
        out_shape=jax.ShapeDtypeStruct((n_rows, d), v.dtype),
        compiler_params=_params(),
        name="moe_dispatch",
    )(pos, pad, v)


def _experts_kernel(te_ref, tv_ref, x_ref, wgu_ref, bgu_ref, wdn_ref, bdn_ref, y_ref, wgu_bf, wdn_bf):
    i = pl.program_id(0)
    half = x_ref.shape[1]

    @pl.when(jnp.logical_or(i == 0, te_ref[i] != te_ref[jnp.maximum(i - 1, 0)]))
    def _():
        wgu_bf[...] = wgu_ref[0].astype(BF16)
        wdn_bf[...] = wdn_ref[0].astype(BF16)

    @pl.when(tv_ref[i] > 0)
    def _():
        de = wdn_ref.shape[1]
        lo, hi = _unpack_pair(x_ref[...])
        hgu = (jnp.dot(lo.astype(BF16), wgu_bf[:half, :], preferred_element_type=F32)
               + jnp.dot(hi.astype(BF16), wgu_bf[half:, :], preferred_element_type=F32) + bgu_ref[0])
        g_lin = jnp.minimum(hgu[:, :de], SWIGLU_LIMIT)
        up = jnp.clip(hgu[:, de:], -SWIGLU_LIMIT, SWIGLU_LIMIT)
        act = (g_lin * jax.nn.sigmoid(SWIGLU_ALPHA * g_lin) * (up + 1.0)).astype(BF16)
        y_lo = jnp.dot(act, wdn_bf[:, :half], preferred_element_type=F32) + bdn_ref[0, :, :half]
        y_hi = jnp.dot(act, wdn_bf[:, half:], preferred_element_type=F32) + bdn_ref[0, :, half:]
        y_ref[...] = _pack_pair(y_lo, y_hi)

    @pl.when(tv_ref[i] == 0)
    def _():
        y_ref[...] = jnp.zeros(y_ref.shape, y_ref.dtype)


def _experts(xs, tile_e, tile_valid, layer, w_gu, b_gu, w_dn, b_dn):
    n_rows, half = xs.shape
    depth, n_exp, d, two_de = w_gu.shape
    de = two_de // 2
    n_tiles = n_rows // EXPERT_TILE
    b_gu = b_gu.reshape(depth * n_exp, 1, two_de)
    b_dn = b_dn.reshape(depth * n_exp, 1, d)
    return pl.pallas_call(
        _experts_kernel,
        grid_spec=pltpu.PrefetchScalarGridSpec(
            num_scalar_prefetch=2,
            grid=(n_tiles,),
            in_specs=[pl.BlockSpec((EXPERT_TILE, half), lambda i, te, tv: (i, 0)),
                      pl.BlockSpec((None, 1, d, two_de), lambda i, te, tv: (layer, te[i], 0, 0)),
                      pl.BlockSpec((1, 1, two_de), lambda i, te, tv: (layer * n_exp + te[i], 0, 0)),
                      pl.BlockSpec((None, 1, de, d), lambda i, te, tv: (layer, te[i], 0, 0)),
                      pl.BlockSpec((1, 1, d), lambda i, te, tv: (layer * n_exp + te[i], 0, 0))],
            out_specs=pl.BlockSpec((EXPERT_TILE, half), lambda i, te, tv: (i, 0)),
            scratch_shapes=[pltpu.VMEM((d, two_de), BF16), pltpu.VMEM((de, d), BF16)]),
        out_shape=jax.ShapeDtypeStruct((n_rows, half), xs.dtype),
        compiler_params=_params(),
        name="moe_experts",
    )(tile_e, tile_valid, xs, w_gu, b_gu, w_dn, b_dn)


COMBINE_TILE = 256
COMBINE_LANES = 512


def _combine_kernel(pos_ref, ys_ref, tw_ref, gate_ref, h_ref, ng_ref, nsh_ref, nsc_ref, *rest, emit_h):
    if emit_h:
        o_ref, u_ref, buf, sem = rest
        h_new = o_ref.at[0]
    else:
        u_ref, buf, sem, h_new = rest
    n_steps = pl.num_programs(0) * pl.num_programs(1)
    i = pl.program_id(0) * pl.num_programs(1) + pl.program_id(1)
    tc = h_ref.shape[1]
    half = buf.shape[3]

    def gather_rows(first_pair, dst, dst_sem, r0, n):
        for t in range(n):
            for k in range(TOP_K):
                _row_copy(ys_ref, pos_ref[first_pair + t * TOP_K + k], dst.at[k], r0 + t, dst_sem).start()

    def wait_tile(dst, dst_sem):
        for k in range(TOP_K):
            pltpu.make_async_copy(ys_ref.at[pl.ds(0, tc), :], dst.at[k], dst_sem).wait()

    for slot in range(2):
        @pl.when(i % 2 == slot)
        def _(slot=slot):
            _combine_step(slot, i, n_steps, tc, half, gather_rows, wait_tile, buf, sem,
                          tw_ref, gate_ref, h_ref, ng_ref, nsh_ref, nsc_ref, h_new, u_ref)


def _combine_step(slot, i, n_steps, tc, half, gather_rows, wait_tile, buf, sem,
                  tw_ref, gate_ref, h_ref, ng_ref, nsh_ref, nsc_ref, h_new, u_ref):
    cur, nxt = buf.at[slot], buf.at[1 - slot]
    cur_sem, nxt_sem = sem.at[slot], sem.at[1 - slot]

    @pl.when(i == 0)
    def _():
        lax.fori_loop(0, tc, lambda r, c: (gather_rows(r * TOP_K, cur, cur_sem, r, 1), c)[1], 0)

    wait_tile(cur, cur_sem)
    nxt_pairs = jnp.minimum(i + 1, n_steps - 1) * (tc * TOP_K)

    def rows(r, carry):
        r0 = pl.multiple_of(r * SUBLANE, SUBLANE)
        gather_rows(nxt_pairs + r0 * TOP_K, nxt, nxt_sem, r0, SUBLANE)
        tw = tw_ref[0, pl.ds(r0, SUBLANE), :]
        wks = [jnp.broadcast_to(tw[:, k:k + 1], (SUBLANE, COMBINE_LANES)) for k in range(TOP_K)]
        for c in range(half // COMBINE_LANES):
            lanes = slice(c * COMBINE_LANES, (c + 1) * COMBINE_LANES)
            f_lo, f_hi = None, None
            for k in range(TOP_K):
                lo, hi = _unpack_pair(cur[k, pl.ds(r0, SUBLANE), lanes])
                f_lo = lo * wks[k] if f_lo is None else f_lo + lo * wks[k]
                f_hi = hi * wks[k] if f_hi is None else f_hi + hi * wks[k]
            hi_lanes = slice(half + c * COMBINE_LANES, half + (c + 1) * COMBINE_LANES)
            h_new[pl.ds(r0, SUBLANE), lanes] = (h_ref[0, pl.ds(r0, SUBLANE), lanes]
                                                + gate_ref[:, lanes] * f_lo)
            h_new[pl.ds(r0, SUBLANE), hi_lanes] = (h_ref[0, pl.ds(r0, SUBLANE), hi_lanes]
                                                   + gate_ref[:, hi_lanes] * f_hi)
        x = h_new[pl.ds(r0, SUBLANE), :]
        y = _rms(x) * ng_ref[...] * (1.0 + nsc_ref[...]) + nsh_ref[...]
        u_ref[0, pl.ds(r0, SUBLANE), :] = y.astype(u_ref.dtype)
        return carry

    lax.fori_loop(0, tc // SUBLANE, rows, 0, unroll=4)

    @pl.when(i == n_steps - 1)
    def _():
        wait_tile(nxt, nxt_sem)


def _combine(ys, pos, top_w, h, ml, row_of_batch, next_g, next_ml, next_dtype, emit_h):
    b, l, d = h.shape
    n_exp = top_w.shape[-1]
    tc = _tile(l, COMBINE_TILE)
    blk = pl.BlockSpec((1, tc, d), lambda bi, i, p: (bi, i, 0))
    mod = lambda chunk: _mod_spec(chunk, d, lambda bi, i, p: row_of_batch(bi), lambda bi, i, p: 0)
    u_shape = jax.ShapeDtypeStruct((b, l, d), next_dtype)
    scratch = [pltpu.VMEM((2, TOP_K, tc, d // 2), ys.dtype), pltpu.SemaphoreType.DMA((2,))]
    if not emit_h:
        scratch.append(pltpu.VMEM((tc, d), F32))
    out = pl.pallas_call(
        functools.partial(_combine_kernel, emit_h=emit_h),
        grid_spec=pltpu.PrefetchScalarGridSpec(
            num_scalar_prefetch=1,
            grid=(b, l // tc),
            in_specs=[pl.BlockSpec(memory_space=pl.ANY),
                      pl.BlockSpec((1, tc, n_exp), lambda bi, i, p: (bi, i, 0)),
                      mod(5), blk,
                      pl.BlockSpec((1, d), lambda bi, i, p: (0, 0)), mod(0), mod(1)],
            out_specs=[blk, blk] if emit_h else blk,
            scratch_shapes=scratch),
        out_shape=[jax.ShapeDtypeStruct((b, l, d), F32), u_shape] if emit_h else u_shape,
        input_output_aliases={4: 0} if emit_h else {},
        compiler_params=_params(),
        name="moe_combine",
    )(pos, ys, top_w, ml, h, next_g.reshape(1, d), next_ml, next_ml)
    return out if emit_h else (None, out)


def _moe(h, g, ml, row_of_batch, layer, w_r, b_r, w_gu, b_gu, w_dn, b_dn, next_g, next_ml, next_dtype, emit_h):
    b, l, d = h.shape
    n_exp = w_r.shape[-1]
    v, top_w, top_i, rank, tile_counts = _router(h, g, ml, row_of_batch, w_r[layer], b_r[layer])
    per_tile = lambda a: a[..., :TOP_K].reshape(tile_counts.shape[0] * tile_counts.shape[1], -1, TOP_K)
    pos, tile_e, tile_valid, n_tiles, pad = _route_tables(
        per_tile(top_i), per_tile(rank), tile_counts.reshape(-1, n_exp), n_exp)
    xs = _dispatch(v.reshape(b * l, d // 2), pos, pad, n_tiles * EXPERT_TILE)
    ys = _experts(xs, tile_e, tile_valid, layer, w_gu, b_gu, w_dn, b_dn)
    return _combine(ys, pos, top_w, h, ml, row_of_batch, next_g, next_ml, next_dtype, emit_h)


def kernel(x, c, ctx, c_ctx, norm1_g, norm2_g, final_g, mod_a, mod_b, mod_bias, w_o, b_o, diff_w_in, diff_lambda, diff_subln_g, gqa_w_in, gqa_q_norm, gqa_k_norm, conv_pw1, conv_pw1_b, conv_dw, conv_dw_b, conv_ln_g, conv_ln_b, router_w, router_b, exp_w_gu, exp_b_gu, exp_w_dn, exp_b_dn):
    b, s, d = x.shape
    lc = ctx.shape[1]
    depth = mod_a.shape[0]
    rows = s // GRID_W
    cos, sin = _rope_tables(rows)
    ctx_last = max([i for i in range(depth) if i % N_MIXERS in ATTN_KINDS], default=-1)

    n_rows = -(-(b + 1) // SUBLANE) * SUBLANE
    cond = jnp.zeros((n_rows, d), F32).at[:b].set(c).at[b].set(c_ctx)
    m_all = _adaln(cond, mod_a, mod_b, mod_bias).reshape(depth, n_rows, 6, 1, d)
    lat_row = lambda bi: bi
    ctx_row = lambda bi: b

    h_lat, h_ctx = x, ctx
    u_lat = _norm_mod(h_lat, norm1_g[0], m_all[0], 0, lat_row)
    u_ctx = _norm_mod(h_ctx, norm1_g[0], m_all[0], 0, ctx_row)
    for i in range(depth):
        kind = i % N_MIXERS
        j = i // N_MIXERS
        upd_ctx = i < ctx_last
        ml = m_all[i]
        w_o_i = w_o[i].astype(BF16)
        o_ctx = None
        if kind == 0:
            o_lat = _fourier_mix(u_lat)
            if upd_ctx:
                o_ctx = _fourier_mix(u_ctx)
        elif kind == 1:
            lam_init = 0.8 - 0.6 * math.exp(-0.3 * i)
            w_in = diff_w_in[j].astype(BF16)
            qkv_lat = _qkv(u_lat.reshape(b * s, d), w_in, cos, sin, d, d, True).reshape(b, s, 3 * d)
            qkv_ctx = _qkv(u_ctx.reshape(b * lc, d), w_in, cos, sin, d, d, False).reshape(b, lc, 3 * d)
            o_lat = _attn_diff(qkv_lat, [qkv_ctx, qkv_lat], diff_lambda[j], diff_subln_g[j], lam_init, d)
            if upd_ctx:
                o_ctx = _attn_diff(qkv_ctx, [qkv_ctx], diff_lambda[j], diff_subln_g[j], lam_init, d)
        elif kind == 2:
            n_heads = d // HEAD_DIM
            nq, nk = n_heads * HEAD_DIM, n_heads // GQA_GROUP * HEAD_DIM
            w_in = gqa_w_in[j].astype(BF16)
            gains = (gqa_q_norm[j], gqa_k_norm[j])
            qkv_lat = _qkv(u_lat.reshape(b * s, d), w_in, cos, sin, nq, nk, True, gains).reshape(b, s, -1)
            qkv_ctx = _qkv(u_ctx.reshape(b * lc, d), w_in, cos, sin, nq, nk, False, gains).reshape(b, lc, -1)
            o_lat = _attn_gqa(qkv_lat, [qkv_ctx, qkv_lat], n_heads)
            if upd_ctx:
                o_ctx = _attn_gqa(qkv_ctx, [qkv_ctx], n_heads)
        else:
            pw1 = conv_pw1[j].astype(BF16)

            def conv(u):
                bb, ll, _ = u.shape
                hg = _linear_glu(u.reshape(bb * ll, d), pw1, conv_pw1_b[j]).reshape(bb, ll, d)
                return _conv_ln_silu(hg, conv_dw[j], conv_dw_b[j], conv_ln_g[j], conv_ln_b[j])

            o_lat = conv(u_lat)
            if upd_ctx:
                o_ctx = conv(u_ctx)

        moe_w = (i, router_w, router_b, exp_w_gu, exp_b_gu, exp_w_dn, exp_b_dn)
        last = i == depth - 1
        nxt = ((final_g, jnp.zeros_like(ml), F32, False) if last
               else (norm1_g[i + 1], m_all[i + 1], BF16, True))
        h_lat = _linear_residual(o_lat, w_o_i, b_o[i], h_lat, ml, 2, lat_row)
        if upd_ctx:
            h_ctx = _linear_residual(o_ctx, w_o_i, b_o[i], h_ctx, ml, 2, ctx_row)
            h_ctx, u_ctx = _moe(h_ctx, norm2_g[i], ml, ctx_row, *moe_w, *nxt)
        h_lat, u_lat = _moe(h_lat, norm2_g[i], ml, lat_row, *moe_w, *nxt)

    return u_lat
```

```python
import functools
import math

import jax
import jax.numpy as jnp
from jax import lax
from jax.experimental import pallas as pl
from jax.experimental.pallas import tpu as pltpu

F32 = jnp.float32
BF16 = jnp.bfloat16

EPS = 1e-6
HEAD_DIM = 128
ROPE_THETA = 10000.0
GRID_W = 64
FOURIER_GROUPS = 8
GQA_GROUP = 4
TOP_K = 4
SWIGLU_LIMIT = 7.0
SWIGLU_ALPHA = 1.702
N_MIXERS = 4
ATTN_KINDS = (1, 2)

V7X_VMEM_BYTES = 64 * 1024 * 1024
VMEM_LIMIT = V7X_VMEM_BYTES - 8 * 1024 * 1024
LANE = 128
SUBLANE = 8


def _params(*sem):
    return pltpu.CompilerParams(dimension_semantics=sem if sem else None,
                                vmem_limit_bytes=VMEM_LIMIT)


def _tile(n, pref):
    t = min(n, pref)
    assert n % t == 0, (n, pref)
    return t


def _adaln_kernel(c_ref, a_ref, b_ref, bias_ref, o_ref):
    c = c_ref[...]
    s = c * jax.nn.sigmoid(c)
    t = jnp.dot(s, a_ref[0], preferred_element_type=F32)
    o_ref[0] = jnp.dot(t, b_ref[0], preferred_element_type=F32) + bias_ref[0]


def _adaln(cond, mod_a, mod_b, mod_bias):
    depth, d, rank = mod_a.shape
    n = mod_b.shape[-1]
    r = cond.shape[0]
    tn = _tile(n, 2048)
    return pl.pallas_call(
        _adaln_kernel,
        grid=(depth, n // tn),
        in_specs=[pl.BlockSpec((r, d), lambda i, j: (0, 0)),
                  pl.BlockSpec((1, d, rank), lambda i, j: (i, 0, 0)),
                  pl.BlockSpec((1, rank, tn), lambda i, j: (i, 0, j)),
                  pl.BlockSpec((1, 1, tn), lambda i, j: (i, 0, j))],
        out_specs=pl.BlockSpec((1, r, tn), lambda i, j: (i, 0, j)),
        out_shape=jax.ShapeDtypeStruct((depth, r, n), F32),
        compiler_params=_params(),
        name="adaln",
    )(cond, mod_a, mod_b, mod_bias.reshape(depth, 1, n))


def _mod_spec(chunk, width, row_fn, col_fn):
    return pl.BlockSpec((None, None, 1, width),
                        lambda *idx: (row_fn(*idx), chunk, 0, col_fn(*idx)))


def _rms(x):
    return x * lax.rsqrt(jnp.mean(x * x, axis=-1, keepdims=True) + EPS)


def _norm_mod_kernel(h_ref, g_ref, sh_ref, sc_ref, o_ref):
    y = _rms(h_ref[0]) * g_ref[...]
    o_ref[0] = (y * (1.0 + sc_ref[...]) + sh_ref[...]).astype(o_ref.dtype)


def _norm_mod(h, g, ml, chunk0, row_of_batch):
    b, l, d = h.shape
    tr = _tile(l, 256)
    return pl.pallas_call(
        _norm_mod_kernel,
        grid=(b, l // tr),
        in_specs=[pl.BlockSpec((1, tr, d), lambda i, j: (i, j, 0)),
                  pl.BlockSpec((1, d), lambda i, j: (0, 0)),
                  _mod_spec(chunk0, d, lambda i, j: row_of_batch(i), lambda i, j: 0),
                  _mod_spec(chunk0 + 1, d, lambda i, j: row_of_batch(i), lambda i, j: 0)],
        out_specs=pl.BlockSpec((1, tr, d), lambda i, j: (i, j, 0)),
        out_shape=jax.ShapeDtypeStruct((b, l, d), BF16),
        compiler_params=_params(),
        name="norm_mod",
    )(h, g.reshape(1, d), ml, ml)


def _linear_kernel(a_ref, w_ref, o_ref):
    o_ref[0] = jnp.dot(a_ref[...], w_ref[0], preferred_element_type=F32).astype(o_ref.dtype)


def _linear(a, w, out_dtype, tm=512, tn=512):
    m, k = a.shape
    g, _, n = w.shape
    tm, tn = _tile(m, tm), _tile(n, tn)
    return pl.pallas_call(
        _linear_kernel,
        grid=(g, m // tm, n // tn),
        in_specs=[pl.BlockSpec((tm, k), lambda b, i, j: (i, 0)),
                  pl.BlockSpec((1, k, tn), lambda b, i, j: (b, 0, j))],
        out_specs=pl.BlockSpec((1, tm, tn), lambda b, i, j: (b, i, j)),
        out_shape=jax.ShapeDtypeStruct((g, m, n), out_dtype),
        compiler_params=_params(),
        name="linear",
    )(a, w)


def _linear_residual_kernel(a_ref, w_ref, b_ref, gate_ref, h_ref, o_ref):
    acc = jnp.dot(a_ref[0], w_ref[...].astype(BF16), preferred_element_type=F32) + b_ref[...]
    o_ref[0] = h_ref[0] + gate_ref[...] * acc


def _linear_residual(o, w, layer, bias, h, ml, gate_chunk, row_of_batch, in_place):
    b, l, k = o.shape
    n = w.shape[2]
    tm, tn = _tile(l, 1024), _tile(n, 512)
    return pl.pallas_call(
        _linear_residual_kernel,
        grid=(b, l // tm, n // tn),
        in_specs=[pl.BlockSpec((1, tm, k), lambda bi, i, j: (bi, i, 0)),
                  pl.BlockSpec((None, k, tn), lambda bi, i, j: (layer, 0, j)),
                  pl.BlockSpec((1, tn), lambda bi, i, j: (0, j)),
                  _mod_spec(gate_chunk, tn, lambda bi, i, j: row_of_batch(bi), lambda bi, i, j: j),
                  pl.BlockSpec((1, tm, tn), lambda bi, i, j: (bi, i, j))],
        out_specs=pl.BlockSpec((1, tm, tn), lambda bi, i, j: (bi, i, j)),
        out_shape=jax.ShapeDtypeStruct((b, l, n), F32),
        input_output_aliases={4: 0} if in_place else {},
        compiler_params=_params(),
        name="linear_residual",
    )(o, w, bias.reshape(1, n), ml, h)


def _linear_glu_kernel(a_ref, w1_ref, w2_ref, b1_ref, b2_ref, o_ref):
    a = a_ref[...]
    lin = jnp.dot(a, w1_ref[...].astype(BF16), preferred_element_type=F32) + b1_ref[...]
    gate = jnp.dot(a, w2_ref[...].astype(BF16), preferred_element_type=F32) + b2_ref[...]
    o_ref[...] = (lin * jax.nn.sigmoid(gate)).astype(o_ref.dtype)


def _linear_glu(a, w, layer, bias):
    m, k = a.shape
    n = w.shape[2] // 2
    tm, tn = _tile(m, 1024), _tile(n, 256)
    nj = n // tn
    bias = bias.reshape(1, 2 * n)
    return pl.pallas_call(
        _linear_glu_kernel,
        grid=(m // tm, nj),
        in_specs=[pl.BlockSpec((tm, k), lambda i, j: (i, 0)),
                  pl.BlockSpec((None, k, tn), lambda i, j: (layer, 0, j)),
                  pl.BlockSpec((None, k, tn), lambda i, j: (layer, 0, j + nj)),
                  pl.BlockSpec((1, tn), lambda i, j: (0, j)),
                  pl.BlockSpec((1, tn), lambda i, j: (0, j + nj))],
        out_specs=pl.BlockSpec((tm, tn), lambda i, j: (i, j)),
        out_shape=jax.ShapeDtypeStruct((m, n), BF16),
        compiler_params=_params(),
        name="linear_glu",
    )(a, w, w, bias, bias)


def _rope_tables(rows):
    nf = HEAD_DIM // 4
    row = jnp.repeat(jnp.arange(rows, dtype=F32), GRID_W)
    col = jnp.tile(jnp.arange(GRID_W, dtype=F32), rows)
    inv = ROPE_THETA ** (-jnp.arange(nf, dtype=F32) / nf)
    ar, ac = row[:, None] * inv, col[:, None] * inv
    cos = jnp.concatenate([jnp.cos(ar), jnp.cos(ar), jnp.cos(ac), jnp.cos(ac)], axis=1)
    sin = jnp.concatenate([-jnp.sin(ar), jnp.sin(ar), -jnp.sin(ac), jnp.sin(ac)], axis=1)
    return cos, sin


def _rope(x, cos, sin):
    nf = HEAD_DIM // 4
    lane = lax.broadcasted_iota(jnp.int32, x.shape, 1)
    first = (lane % (2 * nf)) < nf
    partner = jnp.where(first, pltpu.roll(x, HEAD_DIM - nf, 1), pltpu.roll(x, nf, 1))
    return x * cos + partner * sin


def _qkv_diff_kernel(a_ref, w_ref, cos_ref, sin_ref, o_ref, *, n_q, n_k, rope, scale):
    j = pl.program_id(1)
    acc = jnp.dot(a_ref[...], w_ref[...].astype(BF16), preferred_element_type=F32)
    tn = acc.shape[1]

    def store(mult):
        for c in range(tn // HEAD_DIM):
            x = acc[:, c * HEAD_DIM:(c + 1) * HEAD_DIM]
            if rope:
                x = _rope(x, cos_ref[...], sin_ref[...])
            o_ref[:, c * HEAD_DIM:(c + 1) * HEAD_DIM] = (x * mult).astype(o_ref.dtype)

    @pl.when(j < n_q)
    def _():
        store(scale)

    @pl.when(jnp.logical_and(j >= n_q, j < n_q + n_k))
    def _():
        store(1.0)

    @pl.when(j >= n_q + n_k)
    def _():
        o_ref[...] = acc.astype(o_ref.dtype)


def _qkv_gqa_kernel(a_ref, w_ref, cos_ref, sin_ref, qg_ref, kg_ref, o_ref, *, n_q, n_k, rope, scale):
    j = pl.program_id(1)
    acc = jnp.dot(a_ref[...], w_ref[...].astype(BF16), preferred_element_type=F32)
    tn = acc.shape[1]

    def store(g_ref, mult):
        for c in range(tn // HEAD_DIM):
            x = _rms(acc[:, c * HEAD_DIM:(c + 1) * HEAD_DIM]) * g_ref[...]
            if rope:
                x = _rope(x, cos_ref[...], sin_ref[...])
            o_ref[:, c * HEAD_DIM:(c + 1) * HEAD_DIM] = (x * mult).astype(o_ref.dtype)

    @pl.when(j < n_q)
    def _():
        store(qg_ref, scale)

    @pl.when(jnp.logical_and(j >= n_q, j < n_q + n_k))
    def _():
        store(kg_ref, 1.0)

    @pl.when(j >= n_q + n_k)
    def _():
        o_ref[...] = acc.astype(o_ref.dtype)


def _qkv(a, w, layer, cos, sin, n_q_cols, n_k_cols, rope, gains=None):
    m, k = a.shape
    n = w.shape[2]
    s = cos.shape[0]
    tm, tn = _tile(min(m, s), 1024), _tile(n, 512)
    assert n_q_cols % tn == 0 and n_k_cols % tn == 0
    npos = s // tm
    kw = dict(n_q=n_q_cols // tn, n_k=n_k_cols // tn, rope=rope, scale=HEAD_DIM ** -0.5 * math.log2(math.e))
    in_specs = [pl.BlockSpec((tm, k), lambda i, j: (i, 0)),
                pl.BlockSpec((None, k, tn), lambda i, j: (layer, 0, j)),
                pl.BlockSpec((tm, HEAD_DIM), lambda i, j: (i % npos, 0)),
                pl.BlockSpec((tm, HEAD_DIM), lambda i, j: (i % npos, 0))]
    args = [a, w, cos, sin]
    if gains is None:
        kernel = functools.partial(_qkv_diff_kernel, **kw)
    else:
        kernel = functools.partial(_qkv_gqa_kernel, **kw)
        in_specs += [pl.BlockSpec((1, HEAD_DIM), lambda i, j: (0, 0))] * 2
        args += [gains[0].reshape(1, HEAD_DIM), gains[1].reshape(1, HEAD_DIM)]
    return pl.pallas_call(
        kernel,
        grid=(m // tm, n // tn),
        in_specs=in_specs,
        out_specs=pl.BlockSpec((tm, tn), lambda i, j: (i, j)),
        out_shape=jax.ShapeDtypeStruct((m, n), BF16),
        compiler_params=_params(),
        name="qkv",
    )(*args)


DFT_SPLIT = 64


def _dft_cos_sin(n):
    k = jnp.arange(n, dtype=jnp.int32)

    def tables(cols):
        ang = ((k[:, None] * cols[None, :]) % n).astype(F32) * (2.0 * math.pi / n)
        return jnp.cos(ang), jnp.sin(ang)

    s = 1.0 / math.sqrt(n)
    if n <= DFT_SPLIT or n % DFT_SPLIT:
        c, sn = tables(k)
        return c * s, sn * s
    ca, sa = tables(jnp.arange(n // DFT_SPLIT, dtype=jnp.int32) * DFT_SPLIT)
    cb, sb = tables(jnp.arange(DFT_SPLIT, dtype=jnp.int32))
    ca, sa = ca[:, :, None] * s, sa[:, :, None] * s
    cb, sb = cb[:, None, :], sb[:, None, :]
    return (ca * cb - sa * sb).reshape(n, n), (sa * cb + ca * sb).reshape(n, n)


def _fourier_channels_kernel(u_ref, w_ref, o_ref, *, groups):
    cg = w_ref.shape[0]
    for g in range(groups):
        x = u_ref[0, :, g * cg:(g + 1) * cg]
        y = jnp.dot(x, w_ref[...], preferred_element_type=F32)
        o_ref[0, 0, :, g * cg:(g + 1) * cg] = y[:, :cg].astype(o_ref.dtype)
        o_ref[0, 1, :, g * cg:(g + 1) * cg] = y[:, cg:].astype(o_ref.dtype)


def _fourier_mix(u):
    b, l, d = u.shape
    cg = d // FOURIER_GROUPS
    cc, sc = _dft_cos_sin(cg)
    wch = jnp.concatenate([cc, -sc], axis=1).astype(BF16)
    tm = _tile(l, 512)
    y = pl.pallas_call(
        functools.partial(_fourier_channels_kernel, groups=FOURIER_GROUPS),
        grid=(b, l // tm),
        in_specs=[pl.BlockSpec((1, tm, d), lambda i, j: (i, j, 0)),
                  pl.BlockSpec((cg, 2 * cg), lambda i, j: (0, 0))],
        out_specs=pl.BlockSpec((1, 2, tm, d), lambda i, j: (i, 0, j, 0)),
        out_shape=jax.ShapeDtypeStruct((b, 2, l, d), BF16),
        compiler_params=_params(),
        name="fourier_channels",
    )(u, wch)
    cl, sl = _dft_cos_sin(l)
    wpos = jnp.concatenate([cl, sl], axis=1).astype(BF16)
    return _linear(wpos, y.reshape(b, 2 * l, d), BF16)


def _softmax_parts(s):
    m = jnp.max(s, axis=-1, keepdims=True)
    p = jnp.exp2(s - m)
    return p, jnp.sum(p, axis=-1, keepdims=True)


def _nt_dot(a, b):
    return lax.dot_general(a, b, (((1,), (1,)), ((), ())), preferred_element_type=F32)


def _attn_diff_kernel(*refs, n_src, tq, lam_init):
    lam_ref, g_ref, q_ref = refs[:3]
    kv_refs = refs[3:3 + 2 * n_src]
    o_ref, kbuf, vbuf = refs[3 + 2 * n_src:]
    off = 0
    for s in range(n_src):
        k_ref, v_ref = kv_refs[2 * s], kv_refs[2 * s + 1]
        n = k_ref.shape[0]
        kbuf[off:off + n, :] = k_ref[...]
        vbuf[off:off + n, :] = v_ref[...]
        off += n
    lp = lam_ref[...]
    lam = (jnp.exp(jnp.sum(lp[0:1] * lp[1:2], axis=-1, keepdims=True))
           - jnp.exp(jnp.sum(lp[2:3] * lp[3:4], axis=-1, keepdims=True)) + lam_init)

    def body(qi, carry):
        r0 = pl.multiple_of(qi * tq, tq)
        q = q_ref[pl.ds(r0, tq), :]
        p0, l0 = _softmax_parts(_nt_dot(q[:, :HEAD_DIM], kbuf[:, :HEAD_DIM]))
        p1, l1 = _softmax_parts(_nt_dot(q[:, HEAD_DIM:], kbuf[:, HEAD_DIM:]))
        a = p0 * (1.0 / l0) - p1 * (lam / l1)
        o = jnp.dot(a.astype(BF16), vbuf[...], preferred_element_type=F32)
        o = _rms(o) * g_ref[...] * (1.0 - lam_init)
        o_ref[pl.ds(r0, tq), :] = o.astype(o_ref.dtype)
        return carry

    n_q = q_ref.shape[0] // tq
    lax.fori_loop(0, n_q, body, 0, unroll=4 if n_q % 4 == 0 else 1)


def _attn_diff(q_src, kv_srcs, lam_p, subln_g, lam_init, d_model):
    b, lq, _ = q_src.shape
    hw = 2 * HEAD_DIM
    nh = d_model // hw
    lk = sum(x.shape[1] for x in kv_srcs)
    tq = _tile(lq, 256)
    in_specs = [pl.BlockSpec((4, HEAD_DIM), lambda bi, h: (0, 0)),
                pl.BlockSpec((1, hw), lambda bi, h: (0, 0)),
                pl.BlockSpec((None, lq, hw), lambda bi, h: (bi, 0, h))]
    args = [lam_p, subln_g.reshape(1, hw), q_src]
    for x in kv_srcs:
        in_specs += [pl.BlockSpec((None, x.shape[1], hw), lambda bi, h: (bi, 0, nh + h)),
                     pl.BlockSpec((None, x.shape[1], hw), lambda bi, h: (bi, 0, 2 * nh + h))]
        args += [x, x]
    return pl.pallas_call(
        functools.partial(_attn_diff_kernel, n_src=len(kv_srcs), tq=tq, lam_init=lam_init),
        grid=(b, nh),
        in_specs=in_specs,
        out_specs=pl.BlockSpec((None, lq, hw), lambda bi, h: (bi, 0, h)),
        out_shape=jax.ShapeDtypeStruct((b, lq, d_model), BF16),
        scratch_shapes=[pltpu.VMEM((lk, hw), BF16), pltpu.VMEM((lk, hw), BF16)],
        compiler_params=_params(),
        name="attn_diff",
    )(*args)


def _attn_gqa_kernel(*refs, n_src, tq, group):
    q_ref = refs[0]
    kv_refs = refs[1:1 + 2 * n_src]
    o_ref, kbuf, vbuf = refs[1 + 2 * n_src:]
    off = 0
    for s in range(n_src):
        k_ref, v_ref = kv_refs[2 * s], kv_refs[2 * s + 1]
        n = k_ref.shape[0]
        kbuf[off:off + n, :] = k_ref[...]
        vbuf[off:off + n, :] = v_ref[...]
        off += n

    def body(qi, carry):
        r0 = pl.multiple_of(qi * tq, tq)
        for hh in range(group):
            q = q_ref[pl.ds(r0, tq), hh * HEAD_DIM:(hh + 1) * HEAD_DIM]
            p, l = _softmax_parts(_nt_dot(q, kbuf[...]))
            o = jnp.dot(p.astype(BF16), vbuf[...], preferred_element_type=F32) * (1.0 / l)
            o_ref[pl.ds(r0, tq), hh * HEAD_DIM:(hh + 1) * HEAD_DIM] = o.astype(o_ref.dtype)
        return carry

    n_q = q_ref.shape[0] // tq
    lax.fori_loop(0, n_q, body, 0, unroll=4 if n_q % 4 == 0 else 1)


def _attn_gqa(q_src, kv_srcs, n_heads):
    b, lq, _ = q_src.shape
    n_kv = n_heads // GQA_GROUP
    gw = GQA_GROUP * HEAD_DIM
    lk = sum(x.shape[1] for x in kv_srcs)
    tq = _tile(lq, 256)
    in_specs = [pl.BlockSpec((None, lq, gw), lambda bi, g: (bi, 0, g))]
    args = [q_src]
    for x in kv_srcs:
        in_specs += [pl.BlockSpec((None, x.shape[1], HEAD_DIM), lambda bi, g: (bi, 0, n_heads + g)),
                     pl.BlockSpec((None, x.shape[1], HEAD_DIM), lambda bi, g: (bi, 0, n_heads + n_kv + g))]
        args += [x, x]
    return pl.pallas_call(
        functools.partial(_attn_gqa_kernel, n_src=len(kv_srcs), tq=tq, group=GQA_GROUP),
        grid=(b, n_kv),
        in_specs=in_specs,
        out_specs=pl.BlockSpec((None, lq, gw), lambda bi, g: (bi, 0, g)),
        out_shape=jax.ShapeDtypeStruct((b, lq, n_heads * HEAD_DIM), BF16),
        scratch_shapes=[pltpu.VMEM((lk, HEAD_DIM), BF16), pltpu.VMEM((lk, HEAD_DIM), BF16)],
        compiler_params=_params(),
        name="attn_gqa",
    )(*args)


CONV_HALO = 16
CONV_ROWS = 8
CONV_LANES = 256


def _conv_kernel(prev_ref, cur_ref, next_ref, dw_ref, dwb_ref, g_ref, b_ref, o_ref, ext, conv, taps8, *, width):
    i = pl.program_id(1)
    last = pl.num_programs(1) - 1
    ts = cur_ref.shape[0]
    pad = (width - 1) // 2
    ext[0:CONV_HALO, :] = jnp.where(i > 0, prev_ref[...].astype(F32), 0.0)
    ext[CONV_HALO:CONV_HALO + ts, :] = cur_ref[...].astype(F32)
    ext[CONV_HALO + ts:, :] = jnp.where(i < last, next_ref[...].astype(F32), 0.0)

    first = CONV_HALO - pad
    n_blk = (first + width - 1) // CONV_ROWS + 1
    sub = lax.broadcasted_iota(jnp.int32, (CONV_ROWS, CONV_LANES), 0)
    for j in range(width):
        taps8[j] = jnp.broadcast_to(dw_ref[j:j + 1, :], taps8.shape[1:])

    def residue_sums(m, lanes):
        r0 = pl.multiple_of(m * CONV_ROWS, CONV_ROWS)
        blocks = [ext[pl.ds(r0 + CONV_ROWS * a, CONV_ROWS), lanes] for a in range(n_blk)]
        sums = []
        for b in range(CONV_ROWS):
            taps = [(a, CONV_ROWS * a + b - first) for a in range(n_blk)
                    if 0 <= CONV_ROWS * a + b - first < width]
            acc = blocks[taps[0][0]] * taps8[taps[0][1], :, lanes]
            for a, j in taps[1:]:
                acc = acc + blocks[a] * taps8[j, :, lanes]
            sums.append(acc)
        return tuple(sums)

    for c in range(ext.shape[1] // CONV_LANES):
        lanes = slice(c * CONV_LANES, (c + 1) * CONV_LANES)

        def conv_block(m, cur, lanes=lanes):
            nxt = residue_sums(m + 1, lanes)
            out = cur[0] + dwb_ref[:, lanes]
            for b in range(1, CONV_ROWS):
                mixed = jnp.where(sub >= b, cur[b], nxt[b])
                out = out + pltpu.roll(mixed, CONV_ROWS - b, 0)
            conv[pl.ds(pl.multiple_of(m * CONV_ROWS, CONV_ROWS), CONV_ROWS), lanes] = out
            return nxt

        lax.fori_loop(0, ts // CONV_ROWS, conv_block, residue_sums(0, lanes))

    def body(r, carry):
        r0 = pl.multiple_of(r * CONV_ROWS, CONV_ROWS)
        acc = conv[pl.ds(r0, CONV_ROWS), :]
        mu = jnp.mean(acc, axis=-1, keepdims=True)
        xc = acc - mu
        var = jnp.mean(xc * xc, axis=-1, keepdims=True)
        y = xc * lax.rsqrt(var + EPS) * g_ref[...] + b_ref[...]
        o_ref[pl.ds(r0, CONV_ROWS), :] = (y * jax.nn.sigmoid(y)).astype(o_ref.dtype)
        return carry

    lax.fori_loop(0, ts // CONV_ROWS, body, 0)


def _conv_ln_silu(h, dw, dw_b, ln_g, ln_b):
    b, s, d = h.shape
    width = dw.shape[0]
    assert (width - 1) // 2 <= CONV_HALO
    ts = _tile(s, 256)
    nh = ts // CONV_HALO
    n_halo = s // CONV_HALO
    vec = lambda x: x.reshape(1, d)
    vspec = pl.BlockSpec((1, d), lambda bi, i: (0, 0))
    return pl.pallas_call(
        functools.partial(_conv_kernel, width=width),
        grid=(b, s // ts),
        in_specs=[pl.BlockSpec((None, CONV_HALO, d), lambda bi, i: (bi, jnp.maximum(i * nh - 1, 0), 0)),
                  pl.BlockSpec((None, ts, d), lambda bi, i: (bi, i, 0)),
                  pl.BlockSpec((None, CONV_HALO, d), lambda bi, i: (bi, jnp.minimum((i + 1) * nh, n_halo - 1), 0)),
                  pl.BlockSpec((width, d), lambda bi, i: (0, 0)),
                  vspec, vspec, vspec],
        out_specs=pl.BlockSpec((None, ts, d), lambda bi, i: (bi, i, 0)),
        out_shape=jax.ShapeDtypeStruct((b, s, d), BF16),
        scratch_shapes=[pltpu.VMEM((ts + 2 * CONV_HALO, d), F32), pltpu.VMEM((ts, d), F32),
                        pltpu.VMEM((width, CONV_ROWS, d), F32)],
        compiler_params=_params(),
        name="conv_ln_silu",
    )(h, h, h, dw, vec(dw_b), vec(ln_g), vec(ln_b))


def _split_bf16(x):
    hi = x.astype(BF16)
    return hi, (x - hi.astype(F32)).astype(BF16)


def _pack_pair(lo, hi):
    lo_bits = lax.bitcast_convert_type(lo.astype(BF16).astype(F32), jnp.uint32)
    hi_bits = lax.bitcast_convert_type(hi.astype(BF16).astype(F32), jnp.uint32)
    return (lo_bits >> 16) | (hi_bits & jnp.uint32(0xFFFF0000))


def _unpack_pair(w):
    lo = lax.bitcast_convert_type(w << 16, F32)
    hi = lax.bitcast_convert_type(w & jnp.uint32(0xFFFF0000), F32)
    return lo, hi


def _router_kernel(h_ref, g_ref, sh_ref, sc_ref, wr_ref, br_ref, v_ref, tw_ref, ti_ref, rk_ref, cnt_ref):
    v = _rms(h_ref[0]) * g_ref[...] * (1.0 + sc_ref[...]) + sh_ref[...]
    half = v.shape[1] // 2
    v_ref[0] = _pack_pair(v[:, :half], v[:, half:])
    v_hi, v_lo = _split_bf16(v)
    w_hi, w_lo = _split_bf16(wr_ref[...])
    n_exp = w_hi.shape[1]
    both = jnp.dot(v_hi, jnp.concatenate([w_hi, w_lo], axis=1), preferred_element_type=F32)
    logits = (both[:, :n_exp] + both[:, n_exp:]
              + jnp.dot(v_lo, w_hi, preferred_element_type=F32)) + br_ref[...]
    col = lax.broadcasted_iota(jnp.int32, logits.shape, 1).astype(F32)
    work = logits
    vals, idxs = [], []
    for _ in range(TOP_K):
        m = jnp.max(work, axis=-1, keepdims=True)
        idx = jnp.min(jnp.where(work == m, col, float(n_exp)), axis=-1, keepdims=True)
        vals.append(m)
        idxs.append(idx)
        work = jnp.where(col == idx, -jnp.inf, work)
    exps = [jnp.exp(x - vals[0]) for x in vals]
    denom = exps[0]
    for e in exps[1:]:
        denom = denom + e
    tw = jnp.zeros(logits.shape, F32)
    ti = jnp.zeros(logits.shape, F32)
    selected = jnp.zeros(logits.shape, F32)
    for k in range(TOP_K):
        tw = jnp.where(col == float(k), exps[k] / denom, tw)
        ti = jnp.where(col == float(k), idxs[k], ti)
        selected = jnp.where(col == idxs[k], 1.0, selected)
    tr = logits.shape[0]
    earlier = (lax.broadcasted_iota(jnp.int32, (tr, tr), 1)
               < lax.broadcasted_iota(jnp.int32, (tr, tr), 0)).astype(BF16)
    before = jnp.dot(earlier, selected.astype(BF16), preferred_element_type=F32)
    rk = jnp.zeros(logits.shape, F32)
    for k in range(TOP_K):
        rank_k = jnp.sum(jnp.where(col == idxs[k], before, 0.0), axis=-1, keepdims=True)
        rk = jnp.where(col == float(k), rank_k, rk)
    tw_ref[0] = tw
    ti_ref[0] = ti.astype(jnp.int32)
    rk_ref[0] = rk.astype(jnp.int32)
    cnt_ref[...] = jnp.sum(selected, axis=0, keepdims=True).astype(jnp.int32)


ROUTER_TILE = 256


def _router(h, g, ml, row_of_batch, w_r, b_r):
    b, l, d = h.shape
    n_exp = w_r.shape[1]
    tr = _tile(l, ROUTER_TILE)
    nt = l // tr
    blk = lambda w: pl.BlockSpec((1, tr, w), lambda i, j: (i, j, 0))
    return pl.pallas_call(
        _router_kernel,
        grid=(b, nt),
        in_specs=[blk(d),
                  pl.BlockSpec((1, d), lambda i, j: (0, 0)),
                  _mod_spec(3, d, lambda i, j: row_of_batch(i), lambda i, j: 0),
                  _mod_spec(4, d, lambda i, j: row_of_batch(i), lambda i, j: 0),
                  pl.BlockSpec((d, n_exp), lambda i, j: (0, 0)),
                  pl.BlockSpec((1, n_exp), lambda i, j: (0, 0))],
        out_specs=[blk(d // 2), blk(n_exp), blk(n_exp), blk(n_exp),
                   pl.BlockSpec((None, None, 1, n_exp), lambda i, j: (i, j, 0, 0))],
        out_shape=[jax.ShapeDtypeStruct((b, l, d // 2), jnp.uint32),
                   jax.ShapeDtypeStruct((b, l, n_exp), F32),
                   jax.ShapeDtypeStruct((b, l, n_exp), jnp.int32),
                   jax.ShapeDtypeStruct((b, l, n_exp), jnp.int32),
                   jax.ShapeDtypeStruct((b, nt, 1, n_exp), jnp.int32)],
        compiler_params=_params(),
        name="router",
    )(h, g.reshape(1, d), ml, ml, w_r, b_r.reshape(1, n_exp))


EXPERT_TILE = 256
DISPATCH_TILE = 256


def _route_tables(top_i, rank, tile_counts, n_exp):
    n_pairs = top_i.size
    counts = jnp.sum(tile_counts, axis=0)
    padded = (counts + EXPERT_TILE - 1) // EXPERT_TILE * EXPERT_TILE
    ends = jnp.cumsum(padded)
    starts = ends - padded
    tile_base = starts[None, :] + jnp.cumsum(tile_counts, axis=0) - tile_counts
    experts = jnp.arange(n_exp, dtype=jnp.int32)
    base = jnp.sum(jnp.where(top_i[..., None] == experts, tile_base[:, None, None, :], 0), axis=-1)
    pos = (base + rank).reshape(-1).astype(jnp.int32)
    n_tiles = (n_pairs + n_exp * (EXPERT_TILE - 1)) // EXPERT_TILE + 1
    tile_start = jnp.arange(n_tiles, dtype=jnp.int32) * EXPERT_TILE
    tile_e = jnp.sum((ends[None, :] <= tile_start[:, None]).astype(jnp.int32), axis=1)
    tile_e = jnp.minimum(tile_e, n_exp - 1)
    tile_valid = (tile_start < ends[-1]).astype(jnp.int32)
    pad = jnp.stack([starts + counts, ends]).astype(jnp.int32)
    return pos, tile_e, tile_valid, n_tiles, pad


def _row_copy(src, src_row, dst, dst_row, sem):
    return pltpu.make_async_copy(src.at[pl.ds(src_row, 1), :], dst.at[pl.ds(dst_row, 1), :], sem)


def _dispatch_kernel(pos_ref, pad_ref, v_ref, xs_ref, zeros, sem, zsem):
    i = pl.program_id(0)
    tt = v_ref.shape[0]
    base = i * (tt * TOP_K)

    def issue(r, carry):
        for k in range(TOP_K):
            _row_copy(v_ref, r, xs_ref, pos_ref[base + r * TOP_K + k], sem).start()
        return carry

    lax.fori_loop(0, tt, issue, 0)

    @pl.when(i == 0)
    def _():
        zeros[...] = jnp.zeros(zeros.shape, zeros.dtype)
        n_exp = pad_ref.shape[1]
        zt = zeros.shape[0]
        n_tiles = xs_ref.shape[0] // zt
        first_free = pad_ref[1, n_exp - 1] // zt

        def tile_copy(t):
            return pltpu.make_async_copy(zeros, xs_ref.at[pl.ds(pl.multiple_of(t * zt, zt), zt), :], zsem)

        def for_each_gap(row_fn, tile_fn):
            def per_expert(e, carry):
                lax.fori_loop(pad_ref[0, e], pad_ref[1, e], lambda r, c: (row_fn(r), c)[1], 0)
                return carry
            lax.fori_loop(0, n_exp, per_expert, 0)
            lax.fori_loop(first_free, n_tiles, lambda t, c: (tile_fn(t), c)[1], 0)

        for_each_gap(lambda r: _row_copy(zeros, 0, xs_ref, r, zsem).start(), lambda t: tile_copy(t).start())
        for_each_gap(lambda r: _row_copy(zeros, 0, xs_ref, r, zsem).wait(), lambda t: tile_copy(t).wait())

    for k in range(TOP_K):
        pltpu.make_async_copy(v_ref, xs_ref.at[pl.ds(0, tt), :], sem).wait()


def _dispatch(v, pos, pad, n_rows):
    t, d = v.shape
    tt = _tile(t, DISPATCH_TILE)
    return pl.pallas_call(
        _dispatch_kernel,
        grid_spec=pltpu.PrefetchScalarGridSpec(
            num_scalar_prefetch=2,
            grid=(t // tt,),
            in_specs=[pl.BlockSpec((tt, d), lambda i, pos_ref, pad_ref: (i, 0))],
            out_specs=pl.BlockSpec(memory_space=pl.ANY),
            scratch_shapes=[pltpu.VMEM((EXPERT_TILE, d), v.dtype),
                            pltpu.SemaphoreType.DMA(()), pltpu.SemaphoreType.DMA(())]),
        out_shape=jax.ShapeDtypeStruct((n_rows, d), v.dtype),
        compiler_params=_params(),
        name="moe_dispatch",
    )(pos, pad, v)


def _experts_kernel(te_ref, tv_ref, x_ref, wgu_ref, bgu_ref, wdn_ref, bdn_ref, y_ref, wgu_bf, wdn_bf):
    i = pl.program_id(0)
    half = x_ref.shape[1]

    @pl.when(jnp.logical_or(i == 0, te_ref[i] != te_ref[jnp.maximum(i - 1, 0)]))
    def _():
        wgu_bf[...] = wgu_ref[0].astype(BF16)
        wdn_bf[...] = wdn_ref[0].astype(BF16)

    @pl.when(tv_ref[i] > 0)
    def _():
        de = wdn_ref.shape[1]
        lo, hi = _unpack_pair(x_ref[...])
        hgu = (jnp.dot(lo.astype(BF16), wgu_bf[:half, :], preferred_element_type=F32)
               + jnp.dot(hi.astype(BF16), wgu_bf[half:, :], preferred_element_type=F32) + bgu_ref[0])
        g_lin = jnp.minimum(hgu[:, :de], SWIGLU_LIMIT)
        up = jnp.clip(hgu[:, de:], -SWIGLU_LIMIT, SWIGLU_LIMIT)
        act = (g_lin * jax.nn.sigmoid(SWIGLU_ALPHA * g_lin) * (up + 1.0)).astype(BF16)
        y_lo = jnp.dot(act, wdn_bf[:, :half], preferred_element_type=F32) + bdn_ref[0, :, :half]
        y_hi = jnp.dot(act, wdn_bf[:, half:], preferred_element_type=F32) + bdn_ref[0, :, half:]
        y_ref[...] = _pack_pair(y_lo, y_hi)

    @pl.when(tv_ref[i] == 0)
    def _():
        y_ref[...] = jnp.zeros(y_ref.shape, y_ref.dtype)


def _experts(xs, tile_e, tile_valid, layer, w_gu, b_gu, w_dn, b_dn):
    n_rows, half = xs.shape
    depth, n_exp, d, two_de = w_gu.shape
    de = two_de // 2
    n_tiles = n_rows // EXPERT_TILE
    b_gu = b_gu.reshape(depth * n_exp, 1, two_de)
    b_dn = b_dn.reshape(depth * n_exp, 1, d)
    return pl.pallas_call(
        _experts_kernel,
        grid_spec=pltpu.PrefetchScalarGridSpec(
            num_scalar_prefetch=2,
            grid=(n_tiles,),
            in_specs=[pl.BlockSpec((EXPERT_TILE, half), lambda i, te, tv: (i, 0)),
                      pl.BlockSpec((None, 1, d, two_de), lambda i, te, tv: (layer, te[i], 0, 0)),
                      pl.BlockSpec((1, 1, two_de), lambda i, te, tv: (layer * n_exp + te[i], 0, 0)),
                      pl.BlockSpec((None, 1, de, d), lambda i, te, tv: (layer, te[i], 0, 0)),
                      pl.BlockSpec((1, 1, d), lambda i, te, tv: (layer * n_exp + te[i], 0, 0))],
            out_specs=pl.BlockSpec((EXPERT_TILE, half), lambda i, te, tv: (i, 0)),
            scratch_shapes=[pltpu.VMEM((d, two_de), BF16), pltpu.VMEM((de, d), BF16)]),
        out_shape=jax.ShapeDtypeStruct((n_rows, half), xs.dtype),
        compiler_params=_params(),
        name="moe_experts",
    )(tile_e, tile_valid, xs, w_gu, b_gu, w_dn, b_dn)


COMBINE_TILE = 256
COMBINE_LANES = 512


def _combine_kernel(pos_ref, ys_ref, tw_ref, gate_ref, h_ref, ng_ref, nsh_ref, nsc_ref, *rest, emit_h):
    if emit_h:
        o_ref, u_ref, buf, sem = rest
        h_new = o_ref.at[0]
    else:
        u_ref, buf, sem, h_new = rest
    n_steps = pl.num_programs(0) * pl.num_programs(1)
    i = pl.program_id(0) * pl.num_programs(1) + pl.program_id(1)
    tc = h_ref.shape[1]
    half = buf.shape[3]

    def gather_rows(first_pair, dst, dst_sem, r0, n):
        for t in range(n):
            for k in range(TOP_K):
                _row_copy(ys_ref, pos_ref[first_pair + t * TOP_K + k], dst.at[k], r0 + t, dst_sem).start()

    def wait_tile(dst, dst_sem):
        for k in range(TOP_K):
            pltpu.make_async_copy(ys_ref.at[pl.ds(0, tc), :], dst.at[k], dst_sem).wait()

    for slot in range(2):
        @pl.when(i % 2 == slot)
        def _(slot=slot):
            _combine_step(slot, i, n_steps, tc, half, gather_rows, wait_tile, buf, sem,
                          tw_ref, gate_ref, h_ref, ng_ref, nsh_ref, nsc_ref, h_new, u_ref)


def _combine_step(slot, i, n_steps, tc, half, gather_rows, wait_tile, buf, sem,
                  tw_ref, gate_ref, h_ref, ng_ref, nsh_ref, nsc_ref, h_new, u_ref):
    cur, nxt = buf.at[slot], buf.at[1 - slot]
    cur_sem, nxt_sem = sem.at[slot], sem.at[1 - slot]

    @pl.when(i == 0)
    def _():
        lax.fori_loop(0, tc, lambda r, c: (gather_rows(r * TOP_K, cur, cur_sem, r, 1), c)[1], 0)

    wait_tile(cur, cur_sem)
    nxt_pairs = jnp.minimum(i + 1, n_steps - 1) * (tc * TOP_K)

    def rows(r, carry):
        r0 = pl.multiple_of(r * SUBLANE, SUBLANE)
        gather_rows(nxt_pairs + r0 * TOP_K, nxt, nxt_sem, r0, SUBLANE)
        tw = tw_ref[0, pl.ds(r0, SUBLANE), :]
        wks = [jnp.broadcast_to(tw[:, k:k + 1], (SUBLANE, COMBINE_LANES)) for k in range(TOP_K)]
        for c in range(half // COMBINE_LANES):
            lanes = slice(c * COMBINE_LANES, (c + 1) * COMBINE_LANES)
            f_lo, f_hi = None, None
            for k in range(TOP_K):
                lo, hi = _unpack_pair(cur[k, pl.ds(r0, SUBLANE), lanes])
                f_lo = lo * wks[k] if f_lo is None else f_lo + lo * wks[k]
                f_hi = hi * wks[k] if f_hi is None else f_hi + hi * wks[k]
            hi_lanes = slice(half + c * COMBINE_LANES, half + (c + 1) * COMBINE_LANES)
            h_new[pl.ds(r0, SUBLANE), lanes] = (h_ref[0, pl.ds(r0, SUBLANE), lanes]
                                                + gate_ref[:, lanes] * f_lo)
            h_new[pl.ds(r0, SUBLANE), hi_lanes] = (h_ref[0, pl.ds(r0, SUBLANE), hi_lanes]
                                                   + gate_ref[:, hi_lanes] * f_hi)
        x = h_new[pl.ds(r0, SUBLANE), :]
        y = _rms(x) * ng_ref[...] * (1.0 + nsc_ref[...]) + nsh_ref[...]
        u_ref[0, pl.ds(r0, SUBLANE), :] = y.astype(u_ref.dtype)
        return carry

    lax.fori_loop(0, tc // SUBLANE, rows, 0, unroll=4)

    @pl.when(i == n_steps - 1)
    def _():
        wait_tile(nxt, nxt_sem)


def _combine(ys, pos, top_w, h, ml, row_of_batch, next_g, next_ml, next_dtype, emit_h):
    b, l, d = h.shape
    n_exp = top_w.shape[-1]
    tc = _tile(l, COMBINE_TILE)
    blk = pl.BlockSpec((1, tc, d), lambda bi, i, p: (bi, i, 0))
    mod = lambda chunk: _mod_spec(chunk, d, lambda bi, i, p: row_of_batch(bi), lambda bi, i, p: 0)
    u_shape = jax.ShapeDtypeStruct((b, l, d), next_dtype)
    scratch = [pltpu.VMEM((2, TOP_K, tc, d // 2), ys.dtype), pltpu.SemaphoreType.DMA((2,))]
    if not emit_h:
        scratch.append(pltpu.VMEM((tc, d), F32))
    out = pl.pallas_call(
        functools.partial(_combine_kernel, emit_h=emit_h),
        grid_spec=pltpu.PrefetchScalarGridSpec(
            num_scalar_prefetch=1,
            grid=(b, l // tc),
            in_specs=[pl.BlockSpec(memory_space=pl.ANY),
                      pl.BlockSpec((1, tc, n_exp), lambda bi, i, p: (bi, i, 0)),
                      mod(5), blk,
                      pl.BlockSpec((1, d), lambda bi, i, p: (0, 0)), mod(0), mod(1)],
            out_specs=[blk, blk] if emit_h else blk,
            scratch_shapes=scratch),
        out_shape=[jax.ShapeDtypeStruct((b, l, d), F32), u_shape] if emit_h else u_shape,
        input_output_aliases={4: 0} if emit_h else {},
        compiler_params=_params(),
        name="moe_combine",
    )(pos, ys, top_w, ml, h, next_g.reshape(1, d), next_ml, next_ml)
    return out if emit_h else (None, out)


def _moe(h, g, ml, row_of_batch, layer, w_r, b_r, w_gu, b_gu, w_dn, b_dn, next_g, next_ml, next_dtype, emit_h):
    b, l, d = h.shape
    n_exp = w_r.shape[-1]
    v, top_w, top_i, rank, tile_counts = _router(h, g, ml, row_of_batch, w_r[layer], b_r[layer])
    per_tile = lambda a: a[..., :TOP_K].reshape(tile_counts.shape[0] * tile_counts.shape[1], -1, TOP_K)
    pos, tile_e, tile_valid, n_tiles, pad = _route_tables(
        per_tile(top_i), per_tile(rank), tile_counts.reshape(-1, n_exp), n_exp)
    xs = _dispatch(v.reshape(b * l, d // 2), pos, pad, n_tiles * EXPERT_TILE)
    ys = _experts(xs, tile_e, tile_valid, layer, w_gu, b_gu, w_dn, b_dn)
    return _combine(ys, pos, top_w, h, ml, row_of_batch, next_g, next_ml, next_dtype, emit_h)


def kernel(x, c, ctx, c_ctx, norm1_g, norm2_g, final_g, mod_a, mod_b, mod_bias, w_o, b_o, diff_w_in, diff_lambda, diff_subln_g, gqa_w_in, gqa_q_norm, gqa_k_norm, conv_pw1, conv_pw1_b, conv_dw, conv_dw_b, conv_ln_g, conv_ln_b, router_w, router_b, exp_w_gu, exp_b_gu, exp_w_dn, exp_b_dn):
    b, s, d = x.shape
    lc = ctx.shape[1]
    depth = mod_a.shape[0]
    rows = s // GRID_W
    cos, sin = _rope_tables(rows)
    ctx_last = max([i for i in range(depth) if i % N_MIXERS in ATTN_KINDS], default=-1)

    n_rows = -(-(b + 1) // SUBLANE) * SUBLANE
    cond = jnp.zeros((n_rows, d), F32).at[:b].set(c).at[b].set(c_ctx)
    m_all = _adaln(cond, mod_a, mod_b, mod_bias).reshape(depth, n_rows, 6, 1, d)
    lat_row = lambda bi: bi
    ctx_row = lambda bi: b

    h_lat, h_ctx = x, ctx
    u_lat = _norm_mod(h_lat, norm1_g[0], m_all[0], 0, lat_row)
    u_ctx = _norm_mod(h_ctx, norm1_g[0], m_all[0], 0, ctx_row)
    for i in range(depth):
        kind = i % N_MIXERS
        j = i // N_MIXERS
        upd_ctx = i < ctx_last
        ml = m_all[i]
        o_ctx = None
        if kind == 0:
            o_lat = _fourier_mix(u_lat)
            if upd_ctx:
                o_ctx = _fourier_mix(u_ctx)
        elif kind == 1:
            lam_init = 0.8 - 0.6 * math.exp(-0.3 * i)
            qkv_lat = _qkv(u_lat.reshape(b * s, d), diff_w_in, j, cos, sin, d, d, True).reshape(b, s, 3 * d)
            qkv_ctx = _qkv(u_ctx.reshape(b * lc, d), diff_w_in, j, cos, sin, d, d, False).reshape(b, lc, 3 * d)
            o_lat = _attn_diff(qkv_lat, [qkv_ctx, qkv_lat], diff_lambda[j], diff_subln_g[j], lam_init, d)
            if upd_ctx:
                o_ctx = _attn_diff(qkv_ctx, [qkv_ctx], diff_lambda[j], diff_subln_g[j], lam_init, d)
        elif kind == 2:
            n_heads = d // HEAD_DIM
            nq, nk = n_heads * HEAD_DIM, n_heads // GQA_GROUP * HEAD_DIM
            gains = (gqa_q_norm[j], gqa_k_norm[j])
            qkv_lat = _qkv(u_lat.reshape(b * s, d), gqa_w_in, j, cos, sin, nq, nk, True, gains).reshape(b, s, -1)
            qkv_ctx = _qkv(u_ctx.reshape(b * lc, d), gqa_w_in, j, cos, sin, nq, nk, False, gains).reshape(b, lc, -1)
            o_lat = _attn_gqa(qkv_lat, [qkv_ctx, qkv_lat], n_heads)
            if upd_ctx:
                o_ctx = _attn_gqa(qkv_ctx, [qkv_ctx], n_heads)
        else:
            def conv(u):
                bb, ll, _ = u.shape
                hg = _linear_glu(u.reshape(bb * ll, d), conv_pw1, j, conv_pw1_b[j]).reshape(bb, ll, d)
                return _conv_ln_silu(hg, conv_dw[j], conv_dw_b[j], conv_ln_g[j], conv_ln_b[j])

            o_lat = conv(u_lat)
            if upd_ctx:
                o_ctx = conv(u_ctx)

        moe_w = (i, router_w, router_b, exp_w_gu, exp_b_gu, exp_w_dn, exp_b_dn)
        last = i == depth - 1
        nxt = ((final_g, jnp.zeros_like(ml), F32, False) if last
               else (norm1_g[i + 1], m_all[i + 1], BF16, True))
        h_lat = _linear_residual(o_lat, w_o, i, b_o[i], h_lat, ml, 2, lat_row, in_place=i > 0)
        if upd_ctx:
            h_ctx = _linear_residual(o_ctx, w_o, i, b_o[i], h_ctx, ml, 2, ctx_row, in_place=i > 0)
            h_ctx, u_ctx = _moe(h_ctx, norm2_g[i], ml, ctx_row, *moe_w, *nxt)
        h_lat, u_lat = _moe(h_lat, norm2_g[i], ml, lat_row, *moe_w, *nxt)

    return u_lat
```

```python
import functools
import math

import jax
import jax.numpy as jnp
from jax import lax
from jax.experimental import pallas as pl
from jax.experimental.pallas import tpu as pltpu

F32 = jnp.float32
BF16 = jnp.bfloat16

EPS = 1e-6
HEAD_DIM = 128
ROPE_THETA = 10000.0
GRID_W = 64
FOURIER_GROUPS = 8
GQA_GROUP = 4
TOP_K = 4
SWIGLU_LIMIT = 7.0
SWIGLU_ALPHA = 1.702
N_MIXERS = 4
ATTN_KINDS = (1, 2)

V7X_VMEM_BYTES = 64 * 1024 * 1024
VMEM_LIMIT = V7X_VMEM_BYTES - 8 * 1024 * 1024
LANE = 128
SUBLANE = 8


def _params(*sem):
    return pltpu.CompilerParams(dimension_semantics=sem if sem else None,
                                vmem_limit_bytes=VMEM_LIMIT)


def _tile(n, pref):
    t = min(n, pref)
    assert n % t == 0, (n, pref)
    return t


def _adaln_kernel(c_ref, a_ref, b_ref, bias_ref, o_ref):
    c = c_ref[...]
    s = c * jax.nn.sigmoid(c)
    t = jnp.dot(s, a_ref[0], preferred_element_type=F32)
    o_ref[0] = jnp.dot(t, b_ref[0], preferred_element_type=F32) + bias_ref[0]


def _adaln(cond, mod_a, mod_b, mod_bias):
    depth, d, rank = mod_a.shape
    n = mod_b.shape[-1]
    r = cond.shape[0]
    tn = _tile(n, 2048)
    return pl.pallas_call(
        _adaln_kernel,
        grid=(depth, n // tn),
        in_specs=[pl.BlockSpec((r, d), lambda i, j: (0, 0)),
                  pl.BlockSpec((1, d, rank), lambda i, j: (i, 0, 0)),
                  pl.BlockSpec((1, rank, tn), lambda i, j: (i, 0, j)),
                  pl.BlockSpec((1, 1, tn), lambda i, j: (i, 0, j))],
        out_specs=pl.BlockSpec((1, r, tn), lambda i, j: (i, 0, j)),
        out_shape=jax.ShapeDtypeStruct((depth, r, n), F32),
        compiler_params=_params(),
        name="adaln",
    )(cond, mod_a, mod_b, mod_bias.reshape(depth, 1, n))


def _mod_spec(chunk, width, row_fn, col_fn):
    return pl.BlockSpec((None, None, 1, width),
                        lambda *idx: (row_fn(*idx), chunk, 0, col_fn(*idx)))


def _rms(x):
    return x * lax.rsqrt(jnp.mean(x * x, axis=-1, keepdims=True) + EPS)


def _norm_mod_kernel(h_ref, g_ref, sh_ref, sc_ref, o_ref):
    y = _rms(h_ref[0]) * g_ref[...]
    o_ref[0] = (y * (1.0 + sc_ref[...]) + sh_ref[...]).astype(o_ref.dtype)


def _norm_mod(h, g, ml, chunk0, row_of_batch):
    b, l, d = h.shape
    tr = _tile(l, 256)
    return pl.pallas_call(
        _norm_mod_kernel,
        grid=(b, l // tr),
        in_specs=[pl.BlockSpec((1, tr, d), lambda i, j: (i, j, 0)),
                  pl.BlockSpec((1, d), lambda i, j: (0, 0)),
                  _mod_spec(chunk0, d, lambda i, j: row_of_batch(i), lambda i, j: 0),
                  _mod_spec(chunk0 + 1, d, lambda i, j: row_of_batch(i), lambda i, j: 0)],
        out_specs=pl.BlockSpec((1, tr, d), lambda i, j: (i, j, 0)),
        out_shape=jax.ShapeDtypeStruct((b, l, d), BF16),
        compiler_params=_params(),
        name="norm_mod",
    )(h, g.reshape(1, d), ml, ml)


def _linear_kernel(a_ref, w_ref, o_ref):
    o_ref[0] = jnp.dot(a_ref[...], w_ref[0], preferred_element_type=F32).astype(o_ref.dtype)


def _linear(a, w, out_dtype, tm=512, tn=512):
    m, k = a.shape
    g, _, n = w.shape
    tm, tn = _tile(m, tm), _tile(n, tn)
    return pl.pallas_call(
        _linear_kernel,
        grid=(g, m // tm, n // tn),
        in_specs=[pl.BlockSpec((tm, k), lambda b, i, j: (i, 0)),
                  pl.BlockSpec((1, k, tn), lambda b, i, j: (b, 0, j))],
        out_specs=pl.BlockSpec((1, tm, tn), lambda b, i, j: (b, i, j)),
        out_shape=jax.ShapeDtypeStruct((g, m, n), out_dtype),
        compiler_params=_params(),
        name="linear",
    )(a, w)


def _linear_residual_kernel(a_ref, w_ref, b_ref, gate_ref, h_ref, o_ref):
    acc = jnp.dot(a_ref[0], w_ref[...].astype(BF16), preferred_element_type=F32) + b_ref[...]
    o_ref[0] = h_ref[0] + gate_ref[...] * acc


def _linear_residual(o, w, layer, bias, h, ml, gate_chunk, row_of_batch, in_place):
    b, l, k = o.shape
    n = w.shape[2]
    tm, tn = _tile(l, 1024), _tile(n, 512)
    return pl.pallas_call(
        _linear_residual_kernel,
        grid=(b, l // tm, n // tn),
        in_specs=[pl.BlockSpec((1, tm, k), lambda bi, i, j: (bi, i, 0)),
                  pl.BlockSpec((None, k, tn), lambda bi, i, j: (layer, 0, j)),
                  pl.BlockSpec((1, tn), lambda bi, i, j: (0, j)),
                  _mod_spec(gate_chunk, tn, lambda bi, i, j: row_of_batch(bi), lambda bi, i, j: j),
                  pl.BlockSpec((1, tm, tn), lambda bi, i, j: (bi, i, j))],
        out_specs=pl.BlockSpec((1, tm, tn), lambda bi, i, j: (bi, i, j)),
        out_shape=jax.ShapeDtypeStruct((b, l, n), F32),
        input_output_aliases={4: 0} if in_place else {},
        compiler_params=_params(),
        name="linear_residual",
    )(o, w, bias.reshape(1, n), ml, h)


def _linear_glu_kernel(a_ref, w1_ref, w2_ref, b1_ref, b2_ref, o_ref):
    a = a_ref[...]
    lin = jnp.dot(a, w1_ref[...].astype(BF16), preferred_element_type=F32) + b1_ref[...]
    gate = jnp.dot(a, w2_ref[...].astype(BF16), preferred_element_type=F32) + b2_ref[...]
    o_ref[...] = (lin * jax.nn.sigmoid(gate)).astype(o_ref.dtype)


def _linear_glu(a, w, layer, bias):
    m, k = a.shape
    n = w.shape[2] // 2
    tm, tn = _tile(m, 1024), _tile(n, 256)
    nj = n // tn
    bias = bias.reshape(1, 2 * n)
    return pl.pallas_call(
        _linear_glu_kernel,
        grid=(m // tm, nj),
        in_specs=[pl.BlockSpec((tm, k), lambda i, j: (i, 0)),
                  pl.BlockSpec((None, k, tn), lambda i, j: (layer, 0, j)),
                  pl.BlockSpec((None, k, tn), lambda i, j: (layer, 0, j + nj)),
                  pl.BlockSpec((1, tn), lambda i, j: (0, j)),
                  pl.BlockSpec((1, tn), lambda i, j: (0, j + nj))],
        out_specs=pl.BlockSpec((tm, tn), lambda i, j: (i, j)),
        out_shape=jax.ShapeDtypeStruct((m, n), BF16),
        compiler_params=_params(),
        name="linear_glu",
    )(a, w, w, bias, bias)


def _rope_tables(rows):
    nf = HEAD_DIM // 4
    row = jnp.repeat(jnp.arange(rows, dtype=F32), GRID_W)
    col = jnp.tile(jnp.arange(GRID_W, dtype=F32), rows)
    inv = ROPE_THETA ** (-jnp.arange(nf, dtype=F32) / nf)
    ar, ac = row[:, None] * inv, col[:, None] * inv
    cos = jnp.concatenate([jnp.cos(ar), jnp.cos(ar), jnp.cos(ac), jnp.cos(ac)], axis=1)
    sin = jnp.concatenate([-jnp.sin(ar), jnp.sin(ar), -jnp.sin(ac), jnp.sin(ac)], axis=1)
    return cos, sin


def _rope(x, cos, sin):
    nf = HEAD_DIM // 4
    lane = lax.broadcasted_iota(jnp.int32, x.shape, 1)
    first = (lane % (2 * nf)) < nf
    partner = jnp.where(first, pltpu.roll(x, HEAD_DIM - nf, 1), pltpu.roll(x, nf, 1))
    return x * cos + partner * sin


def _qkv_diff_kernel(a_ref, w_ref, cos_ref, sin_ref, o_ref, *, n_q, n_k, rope, scale):
    j = pl.program_id(1)
    acc = jnp.dot(a_ref[...], w_ref[...].astype(BF16), preferred_element_type=F32)
    tn = acc.shape[1]

    def store(mult):
        for c in range(tn // HEAD_DIM):
            x = acc[:, c * HEAD_DIM:(c + 1) * HEAD_DIM]
            if rope:
                x = _rope(x, cos_ref[...], sin_ref[...])
            o_ref[:, c * HEAD_DIM:(c + 1) * HEAD_DIM] = (x * mult).astype(o_ref.dtype)

    @pl.when(j < n_q)
    def _():
        store(scale)

    @pl.when(jnp.logical_and(j >= n_q, j < n_q + n_k))
    def _():
        store(1.0)

    @pl.when(j >= n_q + n_k)
    def _():
        o_ref[...] = acc.astype(o_ref.dtype)


def _qkv_gqa_kernel(a_ref, w_ref, cos_ref, sin_ref, qg_ref, kg_ref, o_ref, *, n_q, n_k, rope, scale):
    j = pl.program_id(1)
    acc = jnp.dot(a_ref[...], w_ref[...].astype(BF16), preferred_element_type=F32)
    tn = acc.shape[1]

    def store(g_ref, mult):
        for c in range(tn // HEAD_DIM):
            x = _rms(acc[:, c * HEAD_DIM:(c + 1) * HEAD_DIM]) * g_ref[...]
            if rope:
                x = _rope(x, cos_ref[...], sin_ref[...])
            o_ref[:, c * HEAD_DIM:(c + 1) * HEAD_DIM] = (x * mult).astype(o_ref.dtype)

    @pl.when(j < n_q)
    def _():
        store(qg_ref, scale)

    @pl.when(jnp.logical_and(j >= n_q, j < n_q + n_k))
    def _():
        store(kg_ref, 1.0)

    @pl.when(j >= n_q + n_k)
    def _():
        o_ref[...] = acc.astype(o_ref.dtype)


def _qkv(a, w, layer, cos, sin, n_q_cols, n_k_cols, rope, gains=None):
    m, k = a.shape
    n = w.shape[2]
    s = cos.shape[0]
    tm, tn = _tile(min(m, s), 1024), _tile(n, 512)
    assert n_q_cols % tn == 0 and n_k_cols % tn == 0
    npos = s // tm
    kw = dict(n_q=n_q_cols // tn, n_k=n_k_cols // tn, rope=rope, scale=HEAD_DIM ** -0.5 * math.log2(math.e))
    in_specs = [pl.BlockSpec((tm, k), lambda i, j: (i, 0)),
                pl.BlockSpec((None, k, tn), lambda i, j: (layer, 0, j)),
                pl.BlockSpec((tm, HEAD_DIM), lambda i, j: (i % npos, 0)),
                pl.BlockSpec((tm, HEAD_DIM), lambda i, j: (i % npos, 0))]
    args = [a, w, cos, sin]
    if gains is None:
        kernel = functools.partial(_qkv_diff_kernel, **kw)
    else:
        kernel = functools.partial(_qkv_gqa_kernel, **kw)
        in_specs += [pl.BlockSpec((1, HEAD_DIM), lambda i, j: (0, 0))] * 2
        args += [gains[0].reshape(1, HEAD_DIM), gains[1].reshape(1, HEAD_DIM)]
    return pl.pallas_call(
        kernel,
        grid=(m // tm, n // tn),
        in_specs=in_specs,
        out_specs=pl.BlockSpec((tm, tn), lambda i, j: (i, j)),
        out_shape=jax.ShapeDtypeStruct((m, n), BF16),
        compiler_params=_params(),
        name="qkv",
    )(*args)


DFT_SPLIT = 64


def _dft_cos_sin(n):
    k = jnp.arange(n, dtype=jnp.int32)

    def tables(cols):
        ang = ((k[:, None] * cols[None, :]) % n).astype(F32) * (2.0 * math.pi / n)
        return jnp.cos(ang), jnp.sin(ang)

    s = 1.0 / math.sqrt(n)
    if n <= DFT_SPLIT or n % DFT_SPLIT:
        c, sn = tables(k)
        return c * s, sn * s
    ca, sa = tables(jnp.arange(n // DFT_SPLIT, dtype=jnp.int32) * DFT_SPLIT)
    cb, sb = tables(jnp.arange(DFT_SPLIT, dtype=jnp.int32))
    ca, sa = ca[:, :, None] * s, sa[:, :, None] * s
    cb, sb = cb[:, None, :], sb[:, None, :]
    return (ca * cb - sa * sb).reshape(n, n), (sa * cb + ca * sb).reshape(n, n)


def _fourier_channels_kernel(u_ref, w_ref, o_ref, *, groups):
    cg = w_ref.shape[0]
    for g in range(groups):
        x = u_ref[0, :, g * cg:(g + 1) * cg]
        y = jnp.dot(x, w_ref[...], preferred_element_type=F32)
        o_ref[0, 0, :, g * cg:(g + 1) * cg] = y[:, :cg].astype(o_ref.dtype)
        o_ref[0, 1, :, g * cg:(g + 1) * cg] = y[:, cg:].astype(o_ref.dtype)


def _fourier_mix(u):
    b, l, d = u.shape
    cg = d // FOURIER_GROUPS
    cc, sc = _dft_cos_sin(cg)
    wch = jnp.concatenate([cc, -sc], axis=1).astype(BF16)
    tm = _tile(l, 512)
    y = pl.pallas_call(
        functools.partial(_fourier_channels_kernel, groups=FOURIER_GROUPS),
        grid=(b, l // tm),
        in_specs=[pl.BlockSpec((1, tm, d), lambda i, j: (i, j, 0)),
                  pl.BlockSpec((cg, 2 * cg), lambda i, j: (0, 0))],
        out_specs=pl.BlockSpec((1, 2, tm, d), lambda i, j: (i, 0, j, 0)),
        out_shape=jax.ShapeDtypeStruct((b, 2, l, d), BF16),
        compiler_params=_params(),
        name="fourier_channels",
    )(u, wch)
    cl, sl = _dft_cos_sin(l)
    wpos = jnp.concatenate([cl, sl], axis=1).astype(BF16)
    return _linear(wpos, y.reshape(b, 2 * l, d), BF16)


def _softmax_parts(s):
    m = jnp.max(s, axis=-1, keepdims=True)
    p = jnp.exp2(s - m)
    return p, jnp.sum(p, axis=-1, keepdims=True)


def _nt_dot(a, b):
    return lax.dot_general(a, b, (((1,), (1,)), ((), ())), preferred_element_type=F32)


def _attn_diff_kernel(*refs, n_src, tq, lam_init):
    lam_ref, g_ref, q_ref = refs[:3]
    kv_refs = refs[3:3 + 2 * n_src]
    o_ref, kbuf, vbuf = refs[3 + 2 * n_src:]
    off = 0
    for s in range(n_src):
        k_ref, v_ref = kv_refs[2 * s], kv_refs[2 * s + 1]
        n = k_ref.shape[0]
        kbuf[off:off + n, :] = k_ref[...]
        vbuf[off:off + n, :] = v_ref[...]
        off += n
    lp = lam_ref[...]
    lam = (jnp.exp(jnp.sum(lp[0:1] * lp[1:2], axis=-1, keepdims=True))
           - jnp.exp(jnp.sum(lp[2:3] * lp[3:4], axis=-1, keepdims=True)) + lam_init)

    def body(qi, carry):
        r0 = pl.multiple_of(qi * tq, tq)
        q = q_ref[pl.ds(r0, tq), :]
        p0, l0 = _softmax_parts(_nt_dot(q[:, :HEAD_DIM], kbuf[:, :HEAD_DIM]))
        p1, l1 = _softmax_parts(_nt_dot(q[:, HEAD_DIM:], kbuf[:, HEAD_DIM:]))
        a = p0 * (1.0 / l0) - p1 * (lam / l1)
        o = jnp.dot(a.astype(BF16), vbuf[...], preferred_element_type=F32)
        o = _rms(o) * g_ref[...] * (1.0 - lam_init)
        o_ref[pl.ds(r0, tq), :] = o.astype(o_ref.dtype)
        return carry

    n_q = q_ref.shape[0] // tq
    lax.fori_loop(0, n_q, body, 0, unroll=4 if n_q % 4 == 0 else 1)


def _attn_diff(q_src, kv_srcs, lam_p, subln_g, lam_init, d_model):
    b, lq, _ = q_src.shape
    hw = 2 * HEAD_DIM
    nh = d_model // hw
    lk = sum(x.shape[1] for x in kv_srcs)
    tq = _tile(lq, 256)
    in_specs = [pl.BlockSpec((4, HEAD_DIM), lambda bi, h: (0, 0)),
                pl.BlockSpec((1, hw), lambda bi, h: (0, 0)),
                pl.BlockSpec((None, lq, hw), lambda bi, h: (bi, 0, h))]
    args = [lam_p, subln_g.reshape(1, hw), q_src]
    for x in kv_srcs:
        in_specs += [pl.BlockSpec((None, x.shape[1], hw), lambda bi, h: (bi, 0, nh + h)),
                     pl.BlockSpec((None, x.shape[1], hw), lambda bi, h: (bi, 0, 2 * nh + h))]
        args += [x, x]
    return pl.pallas_call(
        functools.partial(_attn_diff_kernel, n_src=len(kv_srcs), tq=tq, lam_init=lam_init),
        grid=(b, nh),
        in_specs=in_specs,
        out_specs=pl.BlockSpec((None, lq, hw), lambda bi, h: (bi, 0, h)),
        out_shape=jax.ShapeDtypeStruct((b, lq, d_model), BF16),
        scratch_shapes=[pltpu.VMEM((lk, hw), BF16), pltpu.VMEM((lk, hw), BF16)],
        compiler_params=_params(),
        name="attn_diff",
    )(*args)


def _attn_gqa_kernel(*refs, n_src, tq, group):
    q_ref = refs[0]
    kv_refs = refs[1:1 + 2 * n_src]
    o_ref, kbuf, vbuf = refs[1 + 2 * n_src:]
    off = 0
    for s in range(n_src):
        k_ref, v_ref = kv_refs[2 * s], kv_refs[2 * s + 1]
        n = k_ref.shape[0]
        kbuf[off:off + n, :] = k_ref[...]
        vbuf[off:off + n, :] = v_ref[...]
        off += n

    def body(qi, carry):
        r0 = pl.multiple_of(qi * tq, tq)
        for hh in range(group):
            q = q_ref[pl.ds(r0, tq), hh * HEAD_DIM:(hh + 1) * HEAD_DIM]
            p, l = _softmax_parts(_nt_dot(q, kbuf[...]))
            o = jnp.dot(p.astype(BF16), vbuf[...], preferred_element_type=F32) * (1.0 / l)
            o_ref[pl.ds(r0, tq), hh * HEAD_DIM:(hh + 1) * HEAD_DIM] = o.astype(o_ref.dtype)
        return carry

    n_q = q_ref.shape[0] // tq
    lax.fori_loop(0, n_q, body, 0, unroll=4 if n_q % 4 == 0 else 1)


def _attn_gqa(q_src, kv_srcs, n_heads):
    b, lq, _ = q_src.shape
    n_kv = n_heads // GQA_GROUP
    gw = GQA_GROUP * HEAD_DIM
    lk = sum(x.shape[1] for x in kv_srcs)
    tq = _tile(lq, 256)
    in_specs = [pl.BlockSpec((None, lq, gw), lambda bi, g: (bi, 0, g))]
    args = [q_src]
    for x in kv_srcs:
        in_specs += [pl.BlockSpec((None, x.shape[1], HEAD_DIM), lambda bi, g: (bi, 0, n_heads + g)),
                     pl.BlockSpec((None, x.shape[1], HEAD_DIM), lambda bi, g: (bi, 0, n_heads + n_kv + g))]
        args += [x, x]
    return pl.pallas_call(
        functools.partial(_attn_gqa_kernel, n_src=len(kv_srcs), tq=tq, group=GQA_GROUP),
        grid=(b, n_kv),
        in_specs=in_specs,
        out_specs=pl.BlockSpec((None, lq, gw), lambda bi, g: (bi, 0, g)),
        out_shape=jax.ShapeDtypeStruct((b, lq, n_heads * HEAD_DIM), BF16),
        scratch_shapes=[pltpu.VMEM((lk, HEAD_DIM), BF16), pltpu.VMEM((lk, HEAD_DIM), BF16)],
        compiler_params=_params(),
        name="attn_gqa",
    )(*args)


CONV_HALO = 16
CONV_ROWS = 8
CONV_LANES = 256


def _conv_kernel(prev_ref, cur_ref, next_ref, dw_ref, dwb_ref, g_ref, b_ref, o_ref, ext, conv, taps8, *, width):
    i = pl.program_id(1)
    last = pl.num_programs(1) - 1
    ts = cur_ref.shape[0]
    pad = (width - 1) // 2
    ext[0:CONV_HALO, :] = jnp.where(i > 0, prev_ref[...].astype(F32), 0.0)
    ext[CONV_HALO:CONV_HALO + ts, :] = cur_ref[...].astype(F32)
    ext[CONV_HALO + ts:, :] = jnp.where(i < last, next_ref[...].astype(F32), 0.0)

    first = CONV_HALO - pad
    n_blk = (first + width - 1) // CONV_ROWS + 1
    sub = lax.broadcasted_iota(jnp.int32, (CONV_ROWS, CONV_LANES), 0)
    for j in range(width):
        taps8[j] = jnp.broadcast_to(dw_ref[j:j + 1, :], taps8.shape[1:])

    def residue_sums(m, lanes):
        r0 = pl.multiple_of(m * CONV_ROWS, CONV_ROWS)
        blocks = [ext[pl.ds(r0 + CONV_ROWS * a, CONV_ROWS), lanes] for a in range(n_blk)]
        sums = []
        for b in range(CONV_ROWS):
            taps = [(a, CONV_ROWS * a + b - first) for a in range(n_blk)
                    if 0 <= CONV_ROWS * a + b - first < width]
            acc = blocks[taps[0][0]] * taps8[taps[0][1], :, lanes]
            for a, j in taps[1:]:
                acc = acc + blocks[a] * taps8[j, :, lanes]
            sums.append(acc)
        return tuple(sums)

    for c in range(ext.shape[1] // CONV_LANES):
        lanes = slice(c * CONV_LANES, (c + 1) * CONV_LANES)

        def conv_block(m, cur, lanes=lanes):
            nxt = residue_sums(m + 1, lanes)
            out = cur[0] + dwb_ref[:, lanes]
            for b in range(1, CONV_ROWS):
                mixed = jnp.where(sub >= b, cur[b], nxt[b])
                out = out + pltpu.roll(mixed, CONV_ROWS - b, 0)
            conv[pl.ds(pl.multiple_of(m * CONV_ROWS, CONV_ROWS), CONV_ROWS), lanes] = out
            return nxt

        lax.fori_loop(0, ts // CONV_ROWS, conv_block, residue_sums(0, lanes), unroll=2)

    def body(r, carry):
        r0 = pl.multiple_of(r * CONV_ROWS, CONV_ROWS)
        acc = conv[pl.ds(r0, CONV_ROWS), :]
        mu = jnp.mean(acc, axis=-1, keepdims=True)
        xc = acc - mu
        var = jnp.mean(xc * xc, axis=-1, keepdims=True)
        y = xc * lax.rsqrt(var + EPS) * g_ref[...] + b_ref[...]
        o_ref[pl.ds(r0, CONV_ROWS), :] = (y * jax.nn.sigmoid(y)).astype(o_ref.dtype)
        return carry

    lax.fori_loop(0, ts // CONV_ROWS, body, 0)


def _conv_ln_silu(h, dw, dw_b, ln_g, ln_b):
    b, s, d = h.shape
    width = dw.shape[0]
    assert (width - 1) // 2 <= CONV_HALO
    ts = _tile(s, 256)
    nh = ts // CONV_HALO
    n_halo = s // CONV_HALO
    vec = lambda x: x.reshape(1, d)
    vspec = pl.BlockSpec((1, d), lambda bi, i: (0, 0))
    return pl.pallas_call(
        functools.partial(_conv_kernel, width=width),
        grid=(b, s // ts),
        in_specs=[pl.BlockSpec((None, CONV_HALO, d), lambda bi, i: (bi, jnp.maximum(i * nh - 1, 0), 0)),
                  pl.BlockSpec((None, ts, d), lambda bi, i: (bi, i, 0)),
                  pl.BlockSpec((None, CONV_HALO, d), lambda bi, i: (bi, jnp.minimum((i + 1) * nh, n_halo - 1), 0)),
                  pl.BlockSpec((width, d), lambda bi, i: (0, 0)),
                  vspec, vspec, vspec],
        out_specs=pl.BlockSpec((None, ts, d), lambda bi, i: (bi, i, 0)),
        out_shape=jax.ShapeDtypeStruct((b, s, d), BF16),
        scratch_shapes=[pltpu.VMEM((ts + 2 * CONV_HALO, d), F32), pltpu.VMEM((ts, d), F32),
                        pltpu.VMEM((width, CONV_ROWS, d), F32)],
        compiler_params=_params(),
        name="conv_ln_silu",
    )(h, h, h, dw, vec(dw_b), vec(ln_g), vec(ln_b))


def _split_bf16(x):
    hi = x.astype(BF16)
    return hi, (x - hi.astype(F32)).astype(BF16)


def _pack_pair(lo, hi):
    lo_bits = lax.bitcast_convert_type(lo.astype(BF16).astype(F32), jnp.uint32)
    hi_bits = lax.bitcast_convert_type(hi.astype(BF16).astype(F32), jnp.uint32)
    return (lo_bits >> 16) | (hi_bits & jnp.uint32(0xFFFF0000))


def _unpack_pair(w):
    lo = lax.bitcast_convert_type(w << 16, F32)
    hi = lax.bitcast_convert_type(w & jnp.uint32(0xFFFF0000), F32)
    return lo, hi


def _router_kernel(h_ref, g_ref, sh_ref, sc_ref, wr_ref, br_ref, v_ref, tw_ref, ti_ref, rk_ref, cnt_ref):
    v = _rms(h_ref[0]) * g_ref[...] * (1.0 + sc_ref[...]) + sh_ref[...]
    half = v.shape[1] // 2
    v_ref[0] = _pack_pair(v[:, :half], v[:, half:])
    v_hi, v_lo = _split_bf16(v)
    w_hi, w_lo = _split_bf16(wr_ref[...])
    n_exp = w_hi.shape[1]
    both = jnp.dot(v_hi, jnp.concatenate([w_hi, w_lo], axis=1), preferred_element_type=F32)
    logits = (both[:, :n_exp] + both[:, n_exp:]
              + jnp.dot(v_lo, w_hi, preferred_element_type=F32)) + br_ref[...]
    col = lax.broadcasted_iota(jnp.int32, logits.shape, 1).astype(F32)
    work = logits
    vals, idxs = [], []
    for _ in range(TOP_K):
        m = jnp.max(work, axis=-1, keepdims=True)
        idx = jnp.min(jnp.where(work == m, col, float(n_exp)), axis=-1, keepdims=True)
        vals.append(m)
        idxs.append(idx)
        work = jnp.where(col == idx, -jnp.inf, work)
    exps = [jnp.exp(x - vals[0]) for x in vals]
    denom = exps[0]
    for e in exps[1:]:
        denom = denom + e
    tw = jnp.zeros(logits.shape, F32)
    ti = jnp.zeros(logits.shape, F32)
    selected = jnp.zeros(logits.shape, F32)
    for k in range(TOP_K):
        tw = jnp.where(col == float(k), exps[k] / denom, tw)
        ti = jnp.where(col == float(k), idxs[k], ti)
        selected = jnp.where(col == idxs[k], 1.0, selected)
    tr = logits.shape[0]
    earlier = (lax.broadcasted_iota(jnp.int32, (tr, tr), 1)
               < lax.broadcasted_iota(jnp.int32, (tr, tr), 0)).astype(BF16)
    before = jnp.dot(earlier, selected.astype(BF16), preferred_element_type=F32)
    rk = jnp.zeros(logits.shape, F32)
    for k in range(TOP_K):
        rank_k = jnp.sum(jnp.where(col == idxs[k], before, 0.0), axis=-1, keepdims=True)
        rk = jnp.where(col == float(k), rank_k, rk)
    tw_ref[0] = tw
    ti_ref[0] = ti.astype(jnp.int32)
    rk_ref[0] = rk.astype(jnp.int32)
    cnt_ref[...] = jnp.sum(selected, axis=0, keepdims=True).astype(jnp.int32)


ROUTER_TILE = 256


def _router(h, g, ml, row_of_batch, w_r, b_r):
    b, l, d = h.shape
    n_exp = w_r.shape[1]
    tr = _tile(l, ROUTER_TILE)
    nt = l // tr
    blk = lambda w: pl.BlockSpec((1, tr, w), lambda i, j: (i, j, 0))
    return pl.pallas_call(
        _router_kernel,
        grid=(b, nt),
        in_specs=[blk(d),
                  pl.BlockSpec((1, d), lambda i, j: (0, 0)),
                  _mod_spec(3, d, lambda i, j: row_of_batch(i), lambda i, j: 0),
                  _mod_spec(4, d, lambda i, j: row_of_batch(i), lambda i, j: 0),
                  pl.BlockSpec((d, n_exp), lambda i, j: (0, 0)),
                  pl.BlockSpec((1, n_exp), lambda i, j: (0, 0))],
        out_specs=[blk(d // 2), blk(n_exp), blk(n_exp), blk(n_exp),
                   pl.BlockSpec((None, None, 1, n_exp), lambda i, j: (i, j, 0, 0))],
        out_shape=[jax.ShapeDtypeStruct((b, l, d // 2), jnp.uint32),
                   jax.ShapeDtypeStruct((b, l, n_exp), F32),
                   jax.ShapeDtypeStruct((b, l, n_exp), jnp.int32),
                   jax.ShapeDtypeStruct((b, l, n_exp), jnp.int32),
                   jax.ShapeDtypeStruct((b, nt, 1, n_exp), jnp.int32)],
        compiler_params=_params(),
        name="router",
    )(h, g.reshape(1, d), ml, ml, w_r, b_r.reshape(1, n_exp))


EXPERT_TILE = 256


def _route_tables(top_i, rank, tile_counts, n_exp):
    n_pairs = top_i.size
    counts = jnp.sum(tile_counts, axis=0)
    padded = (counts + EXPERT_TILE - 1) // EXPERT_TILE * EXPERT_TILE
    ends = jnp.cumsum(padded)
    starts = ends - padded
    tile_base = starts[None, :] + jnp.cumsum(tile_counts, axis=0) - tile_counts
    experts = jnp.arange(n_exp, dtype=jnp.int32)
    base = jnp.sum(jnp.where(top_i[..., None] == experts, tile_base[:, None, None, :], 0), axis=-1)
    pos = (base + rank).reshape(-1).astype(jnp.int32)
    n_tiles = (n_pairs + n_exp * (EXPERT_TILE - 1)) // EXPERT_TILE + 1
    tile_start = jnp.arange(n_tiles, dtype=jnp.int32) * EXPERT_TILE
    tile_e = jnp.sum((ends[None, :] <= tile_start[:, None]).astype(jnp.int32), axis=1)
    tile_e = jnp.minimum(tile_e, n_exp - 1)
    tile_valid = (tile_start < ends[-1]).astype(jnp.int32)
    return pos, tile_e, tile_valid, n_tiles


def _row_copy(src, src_row, dst, dst_row, sem):
    return pltpu.make_async_copy(src.at[pl.ds(src_row, 1), :], dst.at[pl.ds(dst_row, 1), :], sem)


def _row_sources_kernel(pos_ref, src_ref):
    def clear(r, carry):
        src_ref[r] = 0
        return carry

    lax.fori_loop(0, src_ref.shape[0], clear, 0, unroll=8)

    group = 4 * TOP_K
    assert pos_ref.shape[0] % group == 0

    def put(g, carry):
        p0 = g * group
        rows = [pos_ref[p0 + t] for t in range(group)]
        for t in range(group):
            src_ref[rows[t]] = g * (group // TOP_K) + t // TOP_K
        return carry

    lax.fori_loop(0, pos_ref.shape[0] // group, put, 0)


def _row_sources(pos, n_rows):
    return pl.pallas_call(
        _row_sources_kernel,
        grid_spec=pltpu.PrefetchScalarGridSpec(
            num_scalar_prefetch=1, grid=(1,), in_specs=[],
            out_specs=pl.BlockSpec(memory_space=pltpu.SMEM)),
        out_shape=jax.ShapeDtypeStruct((n_rows,), jnp.int32),
        compiler_params=_params(),
        name="moe_row_sources",
    )(pos)


def _experts_kernel(te_ref, tv_ref, src_ref, v_ref, wgu_ref, bgu_ref, wdn_ref, bdn_ref, y_ref,
                    wgu_bf, wdn_bf, xbuf0, xbuf1, sem):
    i = pl.program_id(0)
    xbuf = (xbuf0, xbuf1)

    @pl.when(jnp.logical_or(i == 0, te_ref[i] != te_ref[jnp.maximum(i - 1, 0)]))
    def _():
        wgu_bf[...] = wgu_ref[0].astype(BF16)
        wdn_bf[...] = wdn_ref[0].astype(BF16)

    for slot in range(2):
        @pl.when(i % 2 == slot)
        def _(slot=slot):
            _experts_step(slot, i, te_ref, tv_ref, src_ref, v_ref, bgu_ref, wdn_ref, bdn_ref, y_ref,
                          wgu_bf, wdn_bf, xbuf, sem)


def _experts_step(slot, i, te_ref, tv_ref, src_ref, v_ref, bgu_ref, wdn_ref, bdn_ref, y_ref,
                  wgu_bf, wdn_bf, xbuf, sem):
    n_steps = pl.num_programs(0)
    tm, half = y_ref.shape
    cur, nxt = xbuf[slot], xbuf[1 - slot]
    cur_sem, nxt_sem = sem.at[slot], sem.at[1 - slot]

    def fetch(tile, dst, dst_sem, r):
        _row_copy(v_ref, src_ref[tile * tm + r], dst, r, dst_sem).start()

    def wait_rows(dst, dst_sem):
        pltpu.make_async_copy(v_ref.at[pl.ds(0, tm), :], dst, dst_sem).wait()

    @pl.when(i == 0)
    def _():
        lax.fori_loop(0, tm, lambda r, c: (fetch(0, cur, cur_sem, r), c)[1], 0)

    wait_rows(cur, cur_sem)
    nxt_tile = jnp.minimum(i + 1, n_steps - 1)

    @pl.when(tv_ref[i] > 0)
    def _():
        for r in range(tm):
            fetch(nxt_tile, nxt, nxt_sem, r)
        de = wdn_ref.shape[1]
        lo, hi = _unpack_pair(cur[...])
        hgu = (jnp.dot(lo.astype(BF16), wgu_bf[:half, :], preferred_element_type=F32)
               + jnp.dot(hi.astype(BF16), wgu_bf[half:, :], preferred_element_type=F32) + bgu_ref[0])
        g_lin = jnp.minimum(hgu[:, :de], SWIGLU_LIMIT)
        up = jnp.clip(hgu[:, de:], -SWIGLU_LIMIT, SWIGLU_LIMIT)
        act = (g_lin * jax.nn.sigmoid(SWIGLU_ALPHA * g_lin) * (up + 1.0)).astype(BF16)
        y_lo = jnp.dot(act, wdn_bf[:, :half], preferred_element_type=F32) + bdn_ref[0, :, :half]
        y_hi = jnp.dot(act, wdn_bf[:, half:], preferred_element_type=F32) + bdn_ref[0, :, half:]
        y_ref[...] = _pack_pair(y_lo, y_hi)

    @pl.when(tv_ref[i] == 0)
    def _():
        lax.fori_loop(0, tm, lambda r, c: (fetch(nxt_tile, nxt, nxt_sem, r), c)[1], 0)
        y_ref[...] = jnp.zeros(y_ref.shape, y_ref.dtype)

    @pl.when(i == n_steps - 1)
    def _():
        wait_rows(nxt, nxt_sem)


def _experts(v, row_src, tile_e, tile_valid, layer, w_gu, b_gu, w_dn, b_dn):
    half = v.shape[1]
    n_rows = row_src.shape[0]
    depth, n_exp, d, two_de = w_gu.shape
    de = two_de // 2
    n_tiles = n_rows // EXPERT_TILE
    b_gu = b_gu.reshape(depth * n_exp, 1, two_de)
    b_dn = b_dn.reshape(depth * n_exp, 1, d)
    return pl.pallas_call(
        _experts_kernel,
        grid_spec=pltpu.PrefetchScalarGridSpec(
            num_scalar_prefetch=3,
            grid=(n_tiles,),
            in_specs=[pl.BlockSpec(memory_space=pl.ANY),
                      pl.BlockSpec((None, 1, d, two_de), lambda i, te, tv, rs: (layer, te[i], 0, 0)),
                      pl.BlockSpec((1, 1, two_de), lambda i, te, tv, rs: (layer * n_exp + te[i], 0, 0)),
                      pl.BlockSpec((None, 1, de, d), lambda i, te, tv, rs: (layer, te[i], 0, 0)),
                      pl.BlockSpec((1, 1, d), lambda i, te, tv, rs: (layer * n_exp + te[i], 0, 0))],
            out_specs=pl.BlockSpec((EXPERT_TILE, half), lambda i, te, tv, rs: (i, 0)),
            scratch_shapes=[pltpu.VMEM((d, two_de), BF16), pltpu.VMEM((de, d), BF16),
                            pltpu.VMEM((EXPERT_TILE, half), v.dtype), pltpu.VMEM((EXPERT_TILE, half), v.dtype),
                            pltpu.SemaphoreType.DMA((2,))]),
        out_shape=jax.ShapeDtypeStruct((n_rows, half), v.dtype),
        compiler_params=_params(),
        name="moe_experts",
    )(tile_e, tile_valid, row_src, v, w_gu, b_gu, w_dn, b_dn)


COMBINE_TILE = 256
COMBINE_LANES = 512


def _combine_kernel(pos_ref, ys_ref, tw_ref, gate_ref, h_ref, ng_ref, nsh_ref, nsc_ref, *rest, emit_h):
    if emit_h:
        o_ref, u_ref, buf0, buf1, sem = rest
        h_new = o_ref.at[0]
    else:
        u_ref, buf0, buf1, sem, h_new = rest
    buf = (buf0, buf1)
    n_steps = pl.num_programs(0) * pl.num_programs(1)
    i = pl.program_id(0) * pl.num_programs(1) + pl.program_id(1)
    tc = h_ref.shape[1]
    half = buf0.shape[2]

    def gather_rows(first_pair, dst, dst_sem, r0, n):
        for t in range(n):
            for k in range(TOP_K):
                _row_copy(ys_ref, pos_ref[first_pair + t * TOP_K + k], dst.at[k], r0 + t, dst_sem).start()

    def wait_tile(dst, dst_sem):
        for k in range(TOP_K):
            pltpu.make_async_copy(ys_ref.at[pl.ds(0, tc), :], dst.at[k], dst_sem).wait()

    for slot in range(2):
        @pl.when(i % 2 == slot)
        def _(slot=slot):
            _combine_step(slot, i, n_steps, tc, half, gather_rows, wait_tile, buf, sem,
                          tw_ref, gate_ref, h_ref, ng_ref, nsh_ref, nsc_ref, h_new, u_ref)


def _combine_step(slot, i, n_steps, tc, half, gather_rows, wait_tile, buf, sem,
                  tw_ref, gate_ref, h_ref, ng_ref, nsh_ref, nsc_ref, h_new, u_ref):
    cur, nxt = buf[slot], buf[1 - slot]
    cur_sem, nxt_sem = sem.at[slot], sem.at[1 - slot]

    @pl.when(i == 0)
    def _():
        lax.fori_loop(0, tc, lambda r, c: (gather_rows(r * TOP_K, cur, cur_sem, r, 1), c)[1], 0)

    wait_tile(cur, cur_sem)
    nxt_pairs = jnp.minimum(i + 1, n_steps - 1) * (tc * TOP_K)

    def rows(r, carry):
        r0 = pl.multiple_of(r * SUBLANE, SUBLANE)
        gather_rows(nxt_pairs + r0 * TOP_K, nxt, nxt_sem, r0, SUBLANE)
        tw = tw_ref[0, pl.ds(r0, SUBLANE), :]
        wks = [jnp.broadcast_to(tw[:, k:k + 1], (SUBLANE, COMBINE_LANES)) for k in range(TOP_K)]
        for c in range(half // COMBINE_LANES):
            lanes = slice(c * COMBINE_LANES, (c + 1) * COMBINE_LANES)
            f_lo, f_hi = None, None
            for k in range(TOP_K):
                lo, hi = _unpack_pair(cur[k, pl.ds(r0, SUBLANE), lanes])
                f_lo = lo * wks[k] if f_lo is None else f_lo + lo * wks[k]
                f_hi = hi * wks[k] if f_hi is None else f_hi + hi * wks[k]
            hi_lanes = slice(half + c * COMBINE_LANES, half + (c + 1) * COMBINE_LANES)
            h_new[pl.ds(r0, SUBLANE), lanes] = (h_ref[0, pl.ds(r0, SUBLANE), lanes]
                                                + gate_ref[:, lanes] * f_lo)
            h_new[pl.ds(r0, SUBLANE), hi_lanes] = (h_ref[0, pl.ds(r0, SUBLANE), hi_lanes]
                                                   + gate_ref[:, hi_lanes] * f_hi)
        x = h_new[pl.ds(r0, SUBLANE), :]
        y = _rms(x) * ng_ref[...] * (1.0 + nsc_ref[...]) + nsh_ref[...]
        u_ref[0, pl.ds(r0, SUBLANE), :] = y.astype(u_ref.dtype)
        return carry

    lax.fori_loop(0, tc // SUBLANE, rows, 0, unroll=4)

    @pl.when(i == n_steps - 1)
    def _():
        wait_tile(nxt, nxt_sem)


def _combine(ys, pos, top_w, h, ml, row_of_batch, next_g, next_ml, next_dtype, emit_h):
    b, l, d = h.shape
    n_exp = top_w.shape[-1]
    tc = _tile(l, COMBINE_TILE)
    blk = pl.BlockSpec((1, tc, d), lambda bi, i, p: (bi, i, 0))
    mod = lambda chunk: _mod_spec(chunk, d, lambda bi, i, p: row_of_batch(bi), lambda bi, i, p: 0)
    u_shape = jax.ShapeDtypeStruct((b, l, d), next_dtype)
    scratch = [pltpu.VMEM((TOP_K, tc, d // 2), ys.dtype), pltpu.VMEM((TOP_K, tc, d // 2), ys.dtype),
               pltpu.SemaphoreType.DMA((2,))]
    if not emit_h:
        scratch.append(pltpu.VMEM((tc, d), F32))
    out = pl.pallas_call(
        functools.partial(_combine_kernel, emit_h=emit_h),
        grid_spec=pltpu.PrefetchScalarGridSpec(
            num_scalar_prefetch=1,
            grid=(b, l // tc),
            in_specs=[pl.BlockSpec(memory_space=pl.ANY),
                      pl.BlockSpec((1, tc, n_exp), lambda bi, i, p: (bi, i, 0)),
                      mod(5), blk,
                      pl.BlockSpec((1, d), lambda bi, i, p: (0, 0)), mod(0), mod(1)],
            out_specs=[blk, blk] if emit_h else blk,
            scratch_shapes=scratch),
        out_shape=[jax.ShapeDtypeStruct((b, l, d), F32), u_shape] if emit_h else u_shape,
        input_output_aliases={4: 0} if emit_h else {},
        compiler_params=_params(),
        name="moe_combine",
    )(pos, ys, top_w, ml, h, next_g.reshape(1, d), next_ml, next_ml)
    return out if emit_h else (None, out)


def _moe(h, g, ml, row_of_batch, layer, w_r, b_r, w_gu, b_gu, w_dn, b_dn, next_g, next_ml, next_dtype, emit_h):
    b, l, d = h.shape
    n_exp = w_r.shape[-1]
    v, top_w, top_i, rank, tile_counts = _router(h, g, ml, row_of_batch, w_r[layer], b_r[layer])
    per_tile = lambda a: a[..., :TOP_K].reshape(tile_counts.shape[0] * tile_counts.shape[1], -1, TOP_K)
    pos, tile_e, tile_valid, n_tiles = _route_tables(
        per_tile(top_i), per_tile(rank), tile_counts.reshape(-1, n_exp), n_exp)
    row_src = _row_sources(pos, n_tiles * EXPERT_TILE)
    ys = _experts(v.reshape(b * l, d // 2), row_src, tile_e, tile_valid, layer, w_gu, b_gu, w_dn, b_dn)
    return _combine(ys, pos, top_w, h, ml, row_of_batch, next_g, next_ml, next_dtype, emit_h)


def kernel(x, c, ctx, c_ctx, norm1_g, norm2_g, final_g, mod_a, mod_b, mod_bias, w_o, b_o, diff_w_in, diff_lambda, diff_subln_g, gqa_w_in, gqa_q_norm, gqa_k_norm, conv_pw1, conv_pw1_b, conv_dw, conv_dw_b, conv_ln_g, conv_ln_b, router_w, router_b, exp_w_gu, exp_b_gu, exp_w_dn, exp_b_dn):
    b, s, d = x.shape
    lc = ctx.shape[1]
    depth = mod_a.shape[0]
    rows = s // GRID_W
    cos, sin = _rope_tables(rows)
    ctx_last = max([i for i in range(depth) if i % N_MIXERS in ATTN_KINDS], default=-1)

    n_rows = -(-(b + 1) // SUBLANE) * SUBLANE
    cond = jnp.zeros((n_rows, d), F32).at[:b].set(c).at[b].set(c_ctx)
    m_all = _adaln(cond, mod_a, mod_b, mod_bias).reshape(depth, n_rows, 6, 1, d)
    lat_row = lambda bi: bi
    ctx_row = lambda bi: b

    h_lat, h_ctx = x, ctx
    u_lat = _norm_mod(h_lat, norm1_g[0], m_all[0], 0, lat_row)
    u_ctx = _norm_mod(h_ctx, norm1_g[0], m_all[0], 0, ctx_row)
    for i in range(depth):
        kind = i % N_MIXERS
        j = i // N_MIXERS
        upd_ctx = i < ctx_last
        ml = m_all[i]
        o_ctx = None
        if kind == 0:
            o_lat = _fourier_mix(u_lat)
            if upd_ctx:
                o_ctx = _fourier_mix(u_ctx)
        elif kind == 1:
            lam_init = 0.8 - 0.6 * math.exp(-0.3 * i)
            qkv_lat = _qkv(u_lat.reshape(b * s, d), diff_w_in, j, cos, sin, d, d, True).reshape(b, s, 3 * d)
            qkv_ctx = _qkv(u_ctx.reshape(b * lc, d), diff_w_in, j, cos, sin, d, d, False).reshape(b, lc, 3 * d)
            o_lat = _attn_diff(qkv_lat, [qkv_ctx, qkv_lat], diff_lambda[j], diff_subln_g[j], lam_init, d)
            if upd_ctx:
                o_ctx = _attn_diff(qkv_ctx, [qkv_ctx], diff_lambda[j], diff_subln_g[j], lam_init, d)
        elif kind == 2:
            n_heads = d // HEAD_DIM
            nq, nk = n_heads * HEAD_DIM, n_heads // GQA_GROUP * HEAD_DIM
            gains = (gqa_q_norm[j], gqa_k_norm[j])
            qkv_lat = _qkv(u_lat.reshape(b * s, d), gqa_w_in, j, cos, sin, nq, nk, True, gains).reshape(b, s, -1)
            qkv_ctx = _qkv(u_ctx.reshape(b * lc, d), gqa_w_in, j, cos, sin, nq, nk, False, gains).reshape(b, lc, -1)
            o_lat = _attn_gqa(qkv_lat, [qkv_ctx, qkv_lat], n_heads)
            if upd_ctx:
                o_ctx = _attn_gqa(qkv_ctx, [qkv_ctx], n_heads)
        else:
            def conv(u):
                bb, ll, _ = u.shape
                hg = _linear_glu(u.reshape(bb * ll, d), conv_pw1, j, conv_pw1_b[j]).reshape(bb, ll, d)
                return _conv_ln_silu(hg, conv_dw[j], conv_dw_b[j], conv_ln_g[j], conv_ln_b[j])

            o_lat = conv(u_lat)
            if upd_ctx:
                o_ctx = conv(u_ctx)

        moe_w = (i, router_w, router_b, exp_w_gu, exp_b_gu, exp_w_dn, exp_b_dn)
        last = i == depth - 1
        nxt = ((final_g, jnp.zeros_like(ml), F32, False) if last
               else (norm1_g[i + 1], m_all[i + 1], BF16, True))
        h_lat = _linear_residual(o_lat, w_o, i, b_o[i], h_lat, ml, 2, lat_row, in_place=i > 0)
        if upd_ctx:
            h_ctx = _linear_residual(o_ctx, w_o, i, b_o[i], h_ctx, ml, 2, ctx_row, in_place=i > 0)
            h_ctx, u_ctx = _moe(h_ctx, norm2_g[i], ml, ctx_row, *moe_w, *nxt)
        h_lat, u_lat = _moe(h_lat, norm2_g[i], ml, lat_row, *moe_w, *nxt)

    return u_lat
```

```python
import functools
import math

import jax
import jax.numpy as jnp
from jax import lax
from jax.experimental import pallas as pl
from jax.experimental.pallas import tpu as pltpu

F32 = jnp.float32
BF16 = jnp.bfloat16

EPS = 1e-6
HEAD_DIM = 128
ROPE_THETA = 10000.0
GRID_W = 64
FOURIER_GROUPS = 8
GQA_GROUP = 4
TOP_K = 4
SWIGLU_LIMIT = 7.0
SWIGLU_ALPHA = 1.702
N_MIXERS = 4
ATTN_KINDS = (1, 2)

V7X_VMEM_BYTES = 64 * 1024 * 1024
VMEM_LIMIT = V7X_VMEM_BYTES - 8 * 1024 * 1024
LANE = 128
SUBLANE = 8


def _params(*sem):
    return pltpu.CompilerParams(dimension_semantics=sem if sem else None,
                                vmem_limit_bytes=VMEM_LIMIT)


def _tile(n, pref):
    t = min(n, pref)
    assert n % t == 0, (n, pref)
    return t


def _adaln_kernel(c_ref, a_ref, b_ref, bias_ref, o_ref):
    c = c_ref[...]
    s = c * jax.nn.sigmoid(c)
    t = jnp.dot(s, a_ref[0], preferred_element_type=F32)
    o_ref[0] = jnp.dot(t, b_ref[0], preferred_element_type=F32) + bias_ref[0]


def _adaln(cond, mod_a, mod_b, mod_bias):
    depth, d, rank = mod_a.shape
    n = mod_b.shape[-1]
    r = cond.shape[0]
    tn = _tile(n, 2048)
    return pl.pallas_call(
        _adaln_kernel,
        grid=(depth, n // tn),
        in_specs=[pl.BlockSpec((r, d), lambda i, j: (0, 0)),
                  pl.BlockSpec((1, d, rank), lambda i, j: (i, 0, 0)),
                  pl.BlockSpec((1, rank, tn), lambda i, j: (i, 0, j)),
                  pl.BlockSpec((1, 1, tn), lambda i, j: (i, 0, j))],
        out_specs=pl.BlockSpec((1, r, tn), lambda i, j: (i, 0, j)),
        out_shape=jax.ShapeDtypeStruct((depth, r, n), F32),
        compiler_params=_params(),
        name="adaln",
    )(cond, mod_a, mod_b, mod_bias.reshape(depth, 1, n))


def _mod_spec(chunk, width, row_fn, col_fn):
    return pl.BlockSpec((None, None, 1, width),
                        lambda *idx: (row_fn(*idx), chunk, 0, col_fn(*idx)))


def _rms(x):
    return x * lax.rsqrt(jnp.mean(x * x, axis=-1, keepdims=True) + EPS)


def _norm_mod_kernel(h_ref, g_ref, sh_ref, sc_ref, o_ref):
    y = _rms(h_ref[0]) * g_ref[...]
    o_ref[0] = (y * (1.0 + sc_ref[...]) + sh_ref[...]).astype(o_ref.dtype)


def _norm_mod(h, g, ml, chunk0, row_of_batch):
    b, l, d = h.shape
    tr = _tile(l, 256)
    return pl.pallas_call(
        _norm_mod_kernel,
        grid=(b, l // tr),
        in_specs=[pl.BlockSpec((1, tr, d), lambda i, j: (i, j, 0)),
                  pl.BlockSpec((1, d), lambda i, j: (0, 0)),
                  _mod_spec(chunk0, d, lambda i, j: row_of_batch(i), lambda i, j: 0),
                  _mod_spec(chunk0 + 1, d, lambda i, j: row_of_batch(i), lambda i, j: 0)],
        out_specs=pl.BlockSpec((1, tr, d), lambda i, j: (i, j, 0)),
        out_shape=jax.ShapeDtypeStruct((b, l, d), BF16),
        compiler_params=_params(),
        name="norm_mod",
    )(h, g.reshape(1, d), ml, ml)


def _linear_kernel(a_ref, w_ref, o_ref):
    o_ref[0] = jnp.dot(a_ref[...], w_ref[0], preferred_element_type=F32).astype(o_ref.dtype)


def _linear(a, w, out_dtype, tm=512, tn=512):
    m, k = a.shape
    g, _, n = w.shape
    tm, tn = _tile(m, tm), _tile(n, tn)
    return pl.pallas_call(
        _linear_kernel,
        grid=(g, m // tm, n // tn),
        in_specs=[pl.BlockSpec((tm, k), lambda b, i, j: (i, 0)),
                  pl.BlockSpec((1, k, tn), lambda b, i, j: (b, 0, j))],
        out_specs=pl.BlockSpec((1, tm, tn), lambda b, i, j: (b, i, j)),
        out_shape=jax.ShapeDtypeStruct((g, m, n), out_dtype),
        compiler_params=_params(),
        name="linear",
    )(a, w)


def _linear_residual_kernel(a_ref, w_ref, b_ref, gate_ref, h_ref, o_ref):
    acc = jnp.dot(a_ref[0], w_ref[...].astype(BF16), preferred_element_type=F32) + b_ref[...]
    o_ref[0] = h_ref[0] + gate_ref[...] * acc


def _linear_residual(o, w, layer, bias, h, ml, gate_chunk, row_of_batch, in_place):
    b, l, k = o.shape
    n = w.shape[2]
    tm, tn = _tile(l, 1024), _tile(n, 512)
    return pl.pallas_call(
        _linear_residual_kernel,
        grid=(b, l // tm, n // tn),
        in_specs=[pl.BlockSpec((1, tm, k), lambda bi, i, j: (bi, i, 0)),
                  pl.BlockSpec((None, k, tn), lambda bi, i, j: (layer, 0, j)),
                  pl.BlockSpec((1, tn), lambda bi, i, j: (0, j)),
                  _mod_spec(gate_chunk, tn, lambda bi, i, j: row_of_batch(bi), lambda bi, i, j: j),
                  pl.BlockSpec((1, tm, tn), lambda bi, i, j: (bi, i, j))],
        out_specs=pl.BlockSpec((1, tm, tn), lambda bi, i, j: (bi, i, j)),
        out_shape=jax.ShapeDtypeStruct((b, l, n), F32),
        input_output_aliases={4: 0} if in_place else {},
        compiler_params=_params(),
        name="linear_residual",
    )(o, w, bias.reshape(1, n), ml, h)


def _linear_glu_kernel(a_ref, w1_ref, w2_ref, b1_ref, b2_ref, o_ref):
    a = a_ref[...]
    lin = jnp.dot(a, w1_ref[...].astype(BF16), preferred_element_type=F32) + b1_ref[...]
    gate = jnp.dot(a, w2_ref[...].astype(BF16), preferred_element_type=F32) + b2_ref[...]
    o_ref[...] = (lin * jax.nn.sigmoid(gate)).astype(o_ref.dtype)


def _linear_glu(a, w, layer, bias):
    m, k = a.shape
    n = w.shape[2] // 2
    tm, tn = _tile(m, 1024), _tile(n, 256)
    nj = n // tn
    bias = bias.reshape(1, 2 * n)
    return pl.pallas_call(
        _linear_glu_kernel,
        grid=(m // tm, nj),
        in_specs=[pl.BlockSpec((tm, k), lambda i, j: (i, 0)),
                  pl.BlockSpec((None, k, tn), lambda i, j: (layer, 0, j)),
                  pl.BlockSpec((None, k, tn), lambda i, j: (layer, 0, j + nj)),
                  pl.BlockSpec((1, tn), lambda i, j: (0, j)),
                  pl.BlockSpec((1, tn), lambda i, j: (0, j + nj))],
        out_specs=pl.BlockSpec((tm, tn), lambda i, j: (i, j)),
        out_shape=jax.ShapeDtypeStruct((m, n), BF16),
        compiler_params=_params(),
        name="linear_glu",
    )(a, w, w, bias, bias)


def _rope_tables(rows):
    nf = HEAD_DIM // 4
    row = jnp.repeat(jnp.arange(rows, dtype=F32), GRID_W)
    col = jnp.tile(jnp.arange(GRID_W, dtype=F32), rows)
    inv = ROPE_THETA ** (-jnp.arange(nf, dtype=F32) / nf)
    ar, ac = row[:, None] * inv, col[:, None] * inv
    cos = jnp.concatenate([jnp.cos(ar), jnp.cos(ar), jnp.cos(ac), jnp.cos(ac)], axis=1)
    sin = jnp.concatenate([-jnp.sin(ar), jnp.sin(ar), -jnp.sin(ac), jnp.sin(ac)], axis=1)
    return cos, sin


def _rope(x, cos, sin):
    nf = HEAD_DIM // 4
    lane = lax.broadcasted_iota(jnp.int32, x.shape, 1)
    first = (lane % (2 * nf)) < nf
    partner = jnp.where(first, pltpu.roll(x, HEAD_DIM - nf, 1), pltpu.roll(x, nf, 1))
    return x * cos + partner * sin


def _qkv_diff_kernel(a_ref, w_ref, cos_ref, sin_ref, o_ref, *, n_q, n_k, rope, scale):
    j = pl.program_id(1)
    acc = jnp.dot(a_ref[...], w_ref[...].astype(BF16), preferred_element_type=F32)
    tn = acc.shape[1]

    def store(mult):
        for c in range(tn // HEAD_DIM):
            x = acc[:, c * HEAD_DIM:(c + 1) * HEAD_DIM]
            if rope:
                x = _rope(x, cos_ref[...], sin_ref[...])
            o_ref[:, c * HEAD_DIM:(c + 1) * HEAD_DIM] = (x * mult).astype(o_ref.dtype)

    @pl.when(j < n_q)
    def _():
        store(scale)

    @pl.when(jnp.logical_and(j >= n_q, j < n_q + n_k))
    def _():
        store(1.0)

    @pl.when(j >= n_q + n_k)
    def _():
        o_ref[...] = acc.astype(o_ref.dtype)


def _qkv_gqa_kernel(a_ref, w_ref, cos_ref, sin_ref, qg_ref, kg_ref, o_ref, *, n_q, n_k, rope, scale):
    j = pl.program_id(1)
    acc = jnp.dot(a_ref[...], w_ref[...].astype(BF16), preferred_element_type=F32)
    tn = acc.shape[1]

    def store(g_ref, mult):
        for c in range(tn // HEAD_DIM):
            x = _rms(acc[:, c * HEAD_DIM:(c + 1) * HEAD_DIM]) * g_ref[...]
            if rope:
                x = _rope(x, cos_ref[...], sin_ref[...])
            o_ref[:, c * HEAD_DIM:(c + 1) * HEAD_DIM] = (x * mult).astype(o_ref.dtype)

    @pl.when(j < n_q)
    def _():
        store(qg_ref, scale)

    @pl.when(jnp.logical_and(j >= n_q, j < n_q + n_k))
    def _():
        store(kg_ref, 1.0)

    @pl.when(j >= n_q + n_k)
    def _():
        o_ref[...] = acc.astype(o_ref.dtype)


def _qkv(a, w, layer, cos, sin, n_q_cols, n_k_cols, rope, gains=None):
    m, k = a.shape
    n = w.shape[2]
    s = cos.shape[0]
    tm, tn = _tile(min(m, s), 1024), _tile(n, 512)
    assert n_q_cols % tn == 0 and n_k_cols % tn == 0
    npos = s // tm
    kw = dict(n_q=n_q_cols // tn, n_k=n_k_cols // tn, rope=rope, scale=HEAD_DIM ** -0.5 * math.log2(math.e))
    in_specs = [pl.BlockSpec((tm, k), lambda i, j: (i, 0)),
                pl.BlockSpec((None, k, tn), lambda i, j: (layer, 0, j)),
                pl.BlockSpec((tm, HEAD_DIM), lambda i, j: (i % npos, 0)),
                pl.BlockSpec((tm, HEAD_DIM), lambda i, j: (i % npos, 0))]
    args = [a, w, cos, sin]
    if gains is None:
        kernel = functools.partial(_qkv_diff_kernel, **kw)
    else:
        kernel = functools.partial(_qkv_gqa_kernel, **kw)
        in_specs += [pl.BlockSpec((1, HEAD_DIM), lambda i, j: (0, 0))] * 2
        args += [gains[0].reshape(1, HEAD_DIM), gains[1].reshape(1, HEAD_DIM)]
    return pl.pallas_call(
        kernel,
        grid=(m // tm, n // tn),
        in_specs=in_specs,
        out_specs=pl.BlockSpec((tm, tn), lambda i, j: (i, j)),
        out_shape=jax.ShapeDtypeStruct((m, n), BF16),
        compiler_params=_params(),
        name="qkv",
    )(*args)


DFT_SPLIT = 64


def _dft_cos_sin(n):
    k = jnp.arange(n, dtype=jnp.int32)

    def tables(cols):
        ang = ((k[:, None] * cols[None, :]) % n).astype(F32) * (2.0 * math.pi / n)
        return jnp.cos(ang), jnp.sin(ang)

    s = 1.0 / math.sqrt(n)
    if n <= DFT_SPLIT or n % DFT_SPLIT:
        c, sn = tables(k)
        return c * s, sn * s
    ca, sa = tables(jnp.arange(n // DFT_SPLIT, dtype=jnp.int32) * DFT_SPLIT)
    cb, sb = tables(jnp.arange(DFT_SPLIT, dtype=jnp.int32))
    ca, sa = ca[:, :, None] * s, sa[:, :, None] * s
    cb, sb = cb[:, None, :], sb[:, None, :]
    return (ca * cb - sa * sb).reshape(n, n), (sa * cb + ca * sb).reshape(n, n)


def _fourier_channels_kernel(u_ref, w_ref, o_ref, *, groups):
    cg = w_ref.shape[0]
    for g in range(groups):
        x = u_ref[0, :, g * cg:(g + 1) * cg]
        y = jnp.dot(x, w_ref[...], preferred_element_type=F32)
        o_ref[0, 0, :, g * cg:(g + 1) * cg] = y[:, :cg].astype(o_ref.dtype)
        o_ref[0, 1, :, g * cg:(g + 1) * cg] = y[:, cg:].astype(o_ref.dtype)


def _fourier_mix(u):
    b, l, d = u.shape
    cg = d // FOURIER_GROUPS
    cc, sc = _dft_cos_sin(cg)
    wch = jnp.concatenate([cc, -sc], axis=1).astype(BF16)
    tm = _tile(l, 512)
    y = pl.pallas_call(
        functools.partial(_fourier_channels_kernel, groups=FOURIER_GROUPS),
        grid=(b, l // tm),
        in_specs=[pl.BlockSpec((1, tm, d), lambda i, j: (i, j, 0)),
                  pl.BlockSpec((cg, 2 * cg), lambda i, j: (0, 0))],
        out_specs=pl.BlockSpec((1, 2, tm, d), lambda i, j: (i, 0, j, 0)),
        out_shape=jax.ShapeDtypeStruct((b, 2, l, d), BF16),
        compiler_params=_params(),
        name="fourier_channels",
    )(u, wch)
    cl, sl = _dft_cos_sin(l)
    wpos = jnp.concatenate([cl, sl], axis=1).astype(BF16)
    return _linear(wpos, y.reshape(b, 2 * l, d), BF16)


def _softmax_parts(s):
    m = jnp.max(s, axis=-1, keepdims=True)
    p = jnp.exp2(s - m)
    return p, jnp.sum(p, axis=-1, keepdims=True)


def _nt_dot(a, b):
    return lax.dot_general(a, b, (((1,), (1,)), ((), ())), preferred_element_type=F32)


def _attn_diff_kernel(*refs, n_src, tq, lam_init):
    lam_ref, g_ref, q_ref = refs[:3]
    kv_refs = refs[3:3 + 2 * n_src]
    o_ref, kbuf, vbuf = refs[3 + 2 * n_src:]
    off = 0
    for s in range(n_src):
        k_ref, v_ref = kv_refs[2 * s], kv_refs[2 * s + 1]
        n = k_ref.shape[0]
        kbuf[off:off + n, :] = k_ref[...]
        vbuf[off:off + n, :] = v_ref[...]
        off += n
    lp = lam_ref[...]
    lam = (jnp.exp(jnp.sum(lp[0:1] * lp[1:2], axis=-1, keepdims=True))
           - jnp.exp(jnp.sum(lp[2:3] * lp[3:4], axis=-1, keepdims=True)) + lam_init)

    def body(qi, carry):
        r0 = pl.multiple_of(qi * tq, tq)
        q = q_ref[pl.ds(r0, tq), :]
        p0, l0 = _softmax_parts(_nt_dot(q[:, :HEAD_DIM], kbuf[:, :HEAD_DIM]))
        p1, l1 = _softmax_parts(_nt_dot(q[:, HEAD_DIM:], kbuf[:, HEAD_DIM:]))
        a = p0 * (1.0 / l0) - p1 * (lam / l1)
        o = jnp.dot(a.astype(BF16), vbuf[...], preferred_element_type=F32)
        o = _rms(o) * g_ref[...] * (1.0 - lam_init)
        o_ref[pl.ds(r0, tq), :] = o.astype(o_ref.dtype)
        return carry

    n_q = q_ref.shape[0] // tq
    lax.fori_loop(0, n_q, body, 0, unroll=4 if n_q % 4 == 0 else 1)


def _attn_diff(q_src, kv_srcs, lam_p, subln_g, lam_init, d_model):
    b, lq, _ = q_src.shape
    hw = 2 * HEAD_DIM
    nh = d_model // hw
    lk = sum(x.shape[1] for x in kv_srcs)
    tq = _tile(lq, 256)
    in_specs = [pl.BlockSpec((4, HEAD_DIM), lambda bi, h: (0, 0)),
                pl.BlockSpec((1, hw), lambda bi, h: (0, 0)),
                pl.BlockSpec((None, lq, hw), lambda bi, h: (bi, 0, h))]
    args = [lam_p, subln_g.reshape(1, hw), q_src]
    for x in kv_srcs:
        in_specs += [pl.BlockSpec((None, x.shape[1], hw), lambda bi, h: (bi, 0, nh + h)),
                     pl.BlockSpec((None, x.shape[1], hw), lambda bi, h: (bi, 0, 2 * nh + h))]
        args += [x, x]
    return pl.pallas_call(
        functools.partial(_attn_diff_kernel, n_src=len(kv_srcs), tq=tq, lam_init=lam_init),
        grid=(b, nh),
        in_specs=in_specs,
        out_specs=pl.BlockSpec((None, lq, hw), lambda bi, h: (bi, 0, h)),
        out_shape=jax.ShapeDtypeStruct((b, lq, d_model), BF16),
        scratch_shapes=[pltpu.VMEM((lk, hw), BF16), pltpu.VMEM((lk, hw), BF16)],
        compiler_params=_params(),
        name="attn_diff",
    )(*args)


def _attn_gqa_kernel(*refs, n_src, tq, group):
    q_ref = refs[0]
    kv_refs = refs[1:1 + 2 * n_src]
    o_ref, kbuf, vbuf = refs[1 + 2 * n_src:]
    off = 0
    for s in range(n_src):
        k_ref, v_ref = kv_refs[2 * s], kv_refs[2 * s + 1]
        n = k_ref.shape[0]
        kbuf[off:off + n, :] = k_ref[...]
        vbuf[off:off + n, :] = v_ref[...]
        off += n

    def body(qi, carry):
        r0 = pl.multiple_of(qi * tq, tq)
        for hh in range(group):
            q = q_ref[pl.ds(r0, tq), hh * HEAD_DIM:(hh + 1) * HEAD_DIM]
            p, l = _softmax_parts(_nt_dot(q, kbuf[...]))
            o = jnp.dot(p.astype(BF16), vbuf[...], preferred_element_type=F32) * (1.0 / l)
            o_ref[pl.ds(r0, tq), hh * HEAD_DIM:(hh + 1) * HEAD_DIM] = o.astype(o_ref.dtype)
        return carry

    n_q = q_ref.shape[0] // tq
    lax.fori_loop(0, n_q, body, 0, unroll=4 if n_q % 4 == 0 else 1)


def _attn_gqa(q_src, kv_srcs, n_heads):
    b, lq, _ = q_src.shape
    n_kv = n_heads // GQA_GROUP
    gw = GQA_GROUP * HEAD_DIM
    lk = sum(x.shape[1] for x in kv_srcs)
    tq = _tile(lq, 256)
    in_specs = [pl.BlockSpec((None, lq, gw), lambda bi, g: (bi, 0, g))]
    args = [q_src]
    for x in kv_srcs:
        in_specs += [pl.BlockSpec((None, x.shape[1], HEAD_DIM), lambda bi, g: (bi, 0, n_heads + g)),
                     pl.BlockSpec((None, x.shape[1], HEAD_DIM), lambda bi, g: (bi, 0, n_heads + n_kv + g))]
        args += [x, x]
    return pl.pallas_call(
        functools.partial(_attn_gqa_kernel, n_src=len(kv_srcs), tq=tq, group=GQA_GROUP),
        grid=(b, n_kv),
        in_specs=in_specs,
        out_specs=pl.BlockSpec((None, lq, gw), lambda bi, g: (bi, 0, g)),
        out_shape=jax.ShapeDtypeStruct((b, lq, n_heads * HEAD_DIM), BF16),
        scratch_shapes=[pltpu.VMEM((lk, HEAD_DIM), BF16), pltpu.VMEM((lk, HEAD_DIM), BF16)],
        compiler_params=_params(),
        name="attn_gqa",
    )(*args)


CONV_HALO = 16
CONV_ROWS = 8
CONV_LANES = 256


def _conv_kernel(prev_ref, cur_ref, next_ref, dw_ref, dwb_ref, g_ref, b_ref, o_ref, ext, conv, taps8, *, width):
    i = pl.program_id(1)
    last = pl.num_programs(1) - 1
    ts = cur_ref.shape[0]
    pad = (width - 1) // 2
    ext[0:CONV_HALO, :] = jnp.where(i > 0, prev_ref[...].astype(F32), 0.0)
    ext[CONV_HALO:CONV_HALO + ts, :] = cur_ref[...].astype(F32)
    ext[CONV_HALO + ts:, :] = jnp.where(i < last, next_ref[...].astype(F32), 0.0)

    first = CONV_HALO - pad
    n_blk = (first + width - 1) // CONV_ROWS + 1
    sub = lax.broadcasted_iota(jnp.int32, (CONV_ROWS, CONV_LANES), 0)
    for j in range(width):
        taps8[j] = jnp.broadcast_to(dw_ref[j:j + 1, :], taps8.shape[1:])

    def residue_sums(m, lanes):
        r0 = pl.multiple_of(m * CONV_ROWS, CONV_ROWS)
        blocks = [ext[pl.ds(r0 + CONV_ROWS * a, CONV_ROWS), lanes] for a in range(n_blk)]
        sums = []
        for b in range(CONV_ROWS):
            taps = [(a, CONV_ROWS * a + b - first) for a in range(n_blk)
                    if 0 <= CONV_ROWS * a + b - first < width]
            acc = blocks[taps[0][0]] * taps8[taps[0][1], :, lanes]
            for a, j in taps[1:]:
                acc = acc + blocks[a] * taps8[j, :, lanes]
            sums.append(acc)
        return tuple(sums)

    for c in range(ext.shape[1] // CONV_LANES):
        lanes = slice(c * CONV_LANES, (c + 1) * CONV_LANES)

        def conv_block(m, cur, lanes=lanes):
            nxt = residue_sums(m + 1, lanes)
            out = cur[0] + dwb_ref[:, lanes]
            for b in range(1, CONV_ROWS):
                mixed = jnp.where(sub >= b, cur[b], nxt[b])
                out = out + pltpu.roll(mixed, CONV_ROWS - b, 0)
            conv[pl.ds(pl.multiple_of(m * CONV_ROWS, CONV_ROWS), CONV_ROWS), lanes] = out
            return nxt

        lax.fori_loop(0, ts // CONV_ROWS, conv_block, residue_sums(0, lanes), unroll=2)

    def body(r, carry):
        r0 = pl.multiple_of(r * CONV_ROWS, CONV_ROWS)
        acc = conv[pl.ds(r0, CONV_ROWS), :]
        mu = jnp.mean(acc, axis=-1, keepdims=True)
        xc = acc - mu
        var = jnp.mean(xc * xc, axis=-1, keepdims=True)
        y = xc * lax.rsqrt(var + EPS) * g_ref[...] + b_ref[...]
        o_ref[pl.ds(r0, CONV_ROWS), :] = (y * jax.nn.sigmoid(y)).astype(o_ref.dtype)
        return carry

    lax.fori_loop(0, ts // CONV_ROWS, body, 0)


def _conv_ln_silu(h, dw, dw_b, ln_g, ln_b):
    b, s, d = h.shape
    width = dw.shape[0]
    assert (width - 1) // 2 <= CONV_HALO
    ts = _tile(s, 256)
    nh = ts // CONV_HALO
    n_halo = s // CONV_HALO
    vec = lambda x: x.reshape(1, d)
    vspec = pl.BlockSpec((1, d), lambda bi, i: (0, 0))
    return pl.pallas_call(
        functools.partial(_conv_kernel, width=width),
        grid=(b, s // ts),
        in_specs=[pl.BlockSpec((None, CONV_HALO, d), lambda bi, i: (bi, jnp.maximum(i * nh - 1, 0), 0)),
                  pl.BlockSpec((None, ts, d), lambda bi, i: (bi, i, 0)),
                  pl.BlockSpec((None, CONV_HALO, d), lambda bi, i: (bi, jnp.minimum((i + 1) * nh, n_halo - 1), 0)),
                  pl.BlockSpec((width, d), lambda bi, i: (0, 0)),
                  vspec, vspec, vspec],
        out_specs=pl.BlockSpec((None, ts, d), lambda bi, i: (bi, i, 0)),
        out_shape=jax.ShapeDtypeStruct((b, s, d), BF16),
        scratch_shapes=[pltpu.VMEM((ts + 2 * CONV_HALO, d), F32), pltpu.VMEM((ts, d), F32),
                        pltpu.VMEM((width, CONV_ROWS, d), F32)],
        compiler_params=_params(),
        name="conv_ln_silu",
    )(h, h, h, dw, vec(dw_b), vec(ln_g), vec(ln_b))


def _split_bf16(x):
    hi = x.astype(BF16)
    return hi, (x - hi.astype(F32)).astype(BF16)


def _pack_pair(lo, hi):
    lo_bits = lax.bitcast_convert_type(lo.astype(BF16).astype(F32), jnp.uint32)
    hi_bits = lax.bitcast_convert_type(hi.astype(BF16).astype(F32), jnp.uint32)
    return (lo_bits >> 16) | (hi_bits & jnp.uint32(0xFFFF0000))


def _unpack_pair(w):
    lo = lax.bitcast_convert_type(w << 16, F32)
    hi = lax.bitcast_convert_type(w & jnp.uint32(0xFFFF0000), F32)
    return lo, hi


def _router_kernel(h_ref, g_ref, sh_ref, sc_ref, wr_ref, br_ref, v_ref, tw_ref, ti_ref, rk_ref, cnt_ref):
    v = _rms(h_ref[0]) * g_ref[...] * (1.0 + sc_ref[...]) + sh_ref[...]
    half = v.shape[1] // 2
    v_ref[0] = _pack_pair(v[:, :half], v[:, half:])
    v_hi, v_lo = _split_bf16(v)
    w_hi, w_lo = _split_bf16(wr_ref[...])
    n_exp = w_hi.shape[1]
    both = jnp.dot(v_hi, jnp.concatenate([w_hi, w_lo], axis=1), preferred_element_type=F32)
    logits = (both[:, :n_exp] + both[:, n_exp:]
              + jnp.dot(v_lo, w_hi, preferred_element_type=F32)) + br_ref[...]
    col = lax.broadcasted_iota(jnp.int32, logits.shape, 1).astype(F32)
    work = logits
    vals, idxs = [], []
    for _ in range(TOP_K):
        m = jnp.max(work, axis=-1, keepdims=True)
        idx = jnp.min(jnp.where(work == m, col, float(n_exp)), axis=-1, keepdims=True)
        vals.append(m)
        idxs.append(idx)
        work = jnp.where(col == idx, -jnp.inf, work)
    exps = [jnp.exp(x - vals[0]) for x in vals]
    denom = exps[0]
    for e in exps[1:]:
        denom = denom + e
    tw = jnp.zeros(logits.shape, F32)
    ti = jnp.zeros(logits.shape, F32)
    selected = jnp.zeros(logits.shape, F32)
    for k in range(TOP_K):
        tw = jnp.where(col == float(k), exps[k] / denom, tw)
        ti = jnp.where(col == float(k), idxs[k], ti)
        selected = jnp.where(col == idxs[k], 1.0, selected)
    tr = logits.shape[0]
    earlier = (lax.broadcasted_iota(jnp.int32, (tr, tr), 1)
               < lax.broadcasted_iota(jnp.int32, (tr, tr), 0)).astype(BF16)
    before = jnp.dot(earlier, selected.astype(BF16), preferred_element_type=F32)
    rk = jnp.zeros(logits.shape, F32)
    for k in range(TOP_K):
        rank_k = jnp.sum(jnp.where(col == idxs[k], before, 0.0), axis=-1, keepdims=True)
        rk = jnp.where(col == float(k), rank_k, rk)
    tw_ref[0] = tw
    ti_ref[0] = ti.astype(jnp.int32)
    rk_ref[0] = rk.astype(jnp.int32)
    cnt_ref[...] = jnp.sum(selected, axis=0, keepdims=True).astype(jnp.int32)


ROUTER_TILE = 256


def _router(h, g, ml, row_of_batch, w_r, b_r):
    b, l, d = h.shape
    n_exp = w_r.shape[1]
    tr = _tile(l, ROUTER_TILE)
    nt = l // tr
    blk = lambda w: pl.BlockSpec((1, tr, w), lambda i, j: (i, j, 0))
    return pl.pallas_call(
        _router_kernel,
        grid=(b, nt),
        in_specs=[blk(d),
                  pl.BlockSpec((1, d), lambda i, j: (0, 0)),
                  _mod_spec(3, d, lambda i, j: row_of_batch(i), lambda i, j: 0),
                  _mod_spec(4, d, lambda i, j: row_of_batch(i), lambda i, j: 0),
                  pl.BlockSpec((d, n_exp), lambda i, j: (0, 0)),
                  pl.BlockSpec((1, n_exp), lambda i, j: (0, 0))],
        out_specs=[blk(d // 2), blk(n_exp), blk(n_exp), blk(n_exp),
                   pl.BlockSpec((None, None, 1, n_exp), lambda i, j: (i, j, 0, 0))],
        out_shape=[jax.ShapeDtypeStruct((b, l, d // 2), jnp.uint32),
                   jax.ShapeDtypeStruct((b, l, n_exp), F32),
                   jax.ShapeDtypeStruct((b, l, n_exp), jnp.int32),
                   jax.ShapeDtypeStruct((b, l, n_exp), jnp.int32),
                   jax.ShapeDtypeStruct((b, nt, 1, n_exp), jnp.int32)],
        compiler_params=_params(),
        name="router",
    )(h, g.reshape(1, d), ml, ml, w_r, b_r.reshape(1, n_exp))


EXPERT_TILE = 256
DISPATCH_TILE = 256


def _route_tables(top_i, rank, tile_counts, n_exp):
    n_pairs = top_i.size
    counts = jnp.sum(tile_counts, axis=0)
    padded = (counts + EXPERT_TILE - 1) // EXPERT_TILE * EXPERT_TILE
    ends = jnp.cumsum(padded)
    starts = ends - padded
    tile_base = starts[None, :] + jnp.cumsum(tile_counts, axis=0) - tile_counts
    experts = jnp.arange(n_exp, dtype=jnp.int32)
    base = jnp.sum(jnp.where(top_i[..., None] == experts, tile_base[:, None, None, :], 0), axis=-1)
    pos = (base + rank).reshape(-1).astype(jnp.int32)
    n_tiles = (n_pairs + n_exp * (EXPERT_TILE - 1)) // EXPERT_TILE + 1
    tile_start = jnp.arange(n_tiles, dtype=jnp.int32) * EXPERT_TILE
    tile_e = jnp.sum((ends[None, :] <= tile_start[:, None]).astype(jnp.int32), axis=1)
    tile_e = jnp.minimum(tile_e, n_exp - 1)
    tile_valid = (tile_start < ends[-1]).astype(jnp.int32)
    pad = jnp.stack([starts + counts, ends]).astype(jnp.int32)
    return pos, tile_e, tile_valid, n_tiles, pad


def _row_copy(src, src_row, dst, dst_row, sem):
    return pltpu.make_async_copy(src.at[pl.ds(src_row, 1), :], dst.at[pl.ds(dst_row, 1), :], sem)


def _dispatch_kernel(pos_ref, pad_ref, v_ref, xs_ref, zeros, sem, zsem):
    i = pl.program_id(0)
    tt = v_ref.shape[0]
    base = i * (tt * TOP_K)

    def issue(r, carry):
        for k in range(TOP_K):
            _row_copy(v_ref, r, xs_ref, pos_ref[base + r * TOP_K + k], sem).start()
        return carry

    lax.fori_loop(0, tt, issue, 0)

    @pl.when(i == 0)
    def _():
        zeros[...] = jnp.zeros(zeros.shape, zeros.dtype)
        n_exp = pad_ref.shape[1]
        zt = zeros.shape[0]
        n_tiles = xs_ref.shape[0] // zt
        first_free = pad_ref[1, n_exp - 1] // zt

        def tile_copy(t):
            return pltpu.make_async_copy(zeros, xs_ref.at[pl.ds(pl.multiple_of(t * zt, zt), zt), :], zsem)

        def for_each_gap(row_fn, tile_fn):
            def per_expert(e, carry):
                lax.fori_loop(pad_ref[0, e], pad_ref[1, e], lambda r, c: (row_fn(r), c)[1], 0)
                return carry
            lax.fori_loop(0, n_exp, per_expert, 0)
            lax.fori_loop(first_free, n_tiles, lambda t, c: (tile_fn(t), c)[1], 0)

        for_each_gap(lambda r: _row_copy(zeros, 0, xs_ref, r, zsem).start(), lambda t: tile_copy(t).start())
        for_each_gap(lambda r: _row_copy(zeros, 0, xs_ref, r, zsem).wait(), lambda t: tile_copy(t).wait())

    for k in range(TOP_K):
        pltpu.make_async_copy(v_ref, xs_ref.at[pl.ds(0, tt), :], sem).wait()


def _dispatch(v, pos, pad, n_rows):
    t, d = v.shape
    tt = _tile(t, DISPATCH_TILE)
    return pl.pallas_call(
        _dispatch_kernel,
        grid_spec=pltpu.PrefetchScalarGridSpec(
            num_scalar_prefetch=2,
            grid=(t // tt,),
            in_specs=[pl.BlockSpec((tt, d), lambda i, pos_ref, pad_ref: (i, 0))],
            out_specs=pl.BlockSpec(memory_space=pl.ANY),
            scratch_shapes=[pltpu.VMEM((EXPERT_TILE, d), v.dtype),
                            pltpu.SemaphoreType.DMA(()), pltpu.SemaphoreType.DMA(())]),
        out_shape=jax.ShapeDtypeStruct((n_rows, d), v.dtype),
        compiler_params=_params(),
        name="moe_dispatch",
    )(pos, pad, v)


def _experts_kernel(te_ref, tv_ref, x_ref, wgu_ref, bgu_ref, wdn_ref, bdn_ref, y_ref, wgu_bf, wdn_bf):
    i = pl.program_id(0)
    half = x_ref.shape[1]

    @pl.when(jnp.logical_or(i == 0, te_ref[i] != te_ref[jnp.maximum(i - 1, 0)]))
    def _():
        wgu_bf[...] = wgu_ref[0].astype(BF16)
        wdn_bf[...] = wdn_ref[0].astype(BF16)

    @pl.when(tv_ref[i] > 0)
    def _():
        de = wdn_ref.shape[1]
        lo, hi = _unpack_pair(x_ref[...])
        hgu = (jnp.dot(lo.astype(BF16), wgu_bf[:half, :], preferred_element_type=F32)
               + jnp.dot(hi.astype(BF16), wgu_bf[half:, :], preferred_element_type=F32) + bgu_ref[0])
        g_lin = jnp.minimum(hgu[:, :de], SWIGLU_LIMIT)
        up = jnp.clip(hgu[:, de:], -SWIGLU_LIMIT, SWIGLU_LIMIT)
        act = (g_lin * jax.nn.sigmoid(SWIGLU_ALPHA * g_lin) * (up + 1.0)).astype(BF16)
        y_lo = jnp.dot(act, wdn_bf[:, :half], preferred_element_type=F32) + bdn_ref[0, :, :half]
        y_hi = jnp.dot(act, wdn_bf[:, half:], preferred_element_type=F32) + bdn_ref[0, :, half:]
        y_ref[...] = _pack_pair(y_lo, y_hi)

    @pl.when(tv_ref[i] == 0)
    def _():
        y_ref[...] = jnp.zeros(y_ref.shape, y_ref.dtype)


def _experts(xs, tile_e, tile_valid, layer, w_gu, b_gu, w_dn, b_dn):
    n_rows, half = xs.shape
    depth, n_exp, d, two_de = w_gu.shape
    de = two_de // 2
    n_tiles = n_rows // EXPERT_TILE
    b_gu = b_gu.reshape(depth * n_exp, 1, two_de)
    b_dn = b_dn.reshape(depth * n_exp, 1, d)
    return pl.pallas_call(
        _experts_kernel,
        grid_spec=pltpu.PrefetchScalarGridSpec(
            num_scalar_prefetch=2,
            grid=(n_tiles,),
            in_specs=[pl.BlockSpec((EXPERT_TILE, half), lambda i, te, tv: (i, 0)),
                      pl.BlockSpec((None, 1, d, two_de), lambda i, te, tv: (layer, te[i], 0, 0)),
                      pl.BlockSpec((1, 1, two_de), lambda i, te, tv: (layer * n_exp + te[i], 0, 0)),
                      pl.BlockSpec((None, 1, de, d), lambda i, te, tv: (layer, te[i], 0, 0)),
                      pl.BlockSpec((1, 1, d), lambda i, te, tv: (layer * n_exp + te[i], 0, 0))],
            out_specs=pl.BlockSpec((EXPERT_TILE, half), lambda i, te, tv: (i, 0)),
            scratch_shapes=[pltpu.VMEM((d, two_de), BF16), pltpu.VMEM((de, d), BF16)]),
        out_shape=jax.ShapeDtypeStruct((n_rows, half), xs.dtype),
        compiler_params=_params(),
        name="moe_experts",
    )(tile_e, tile_valid, xs, w_gu, b_gu, w_dn, b_dn)


COMBINE_TILE = 256
COMBINE_LANES = 512


def _combine_kernel(pos_ref, ys_ref, tw_ref, gate_ref, h_ref, ng_ref, nsh_ref, nsc_ref, *rest, emit_h):
    if emit_h:
        o_ref, u_ref, buf0, buf1, sem = rest
        h_new = o_ref.at[0]
    else:
        u_ref, buf0, buf1, sem, h_new = rest
    buf = (buf0, buf1)
    n_steps = pl.num_programs(0) * pl.num_programs(1)
    i = pl.program_id(0) * pl.num_programs(1) + pl.program_id(1)
    tc = h_ref.shape[1]
    half = buf0.shape[2]

    def gather_rows(first_pair, dst, dst_sem, r0, n):
        for t in range(n):
            for k in range(TOP_K):
                _row_copy(ys_ref, pos_ref[first_pair + t * TOP_K + k], dst.at[k], r0 + t, dst_sem).start()

    def wait_tile(dst, dst_sem):
        for k in range(TOP_K):
            pltpu.make_async_copy(ys_ref.at[pl.ds(0, tc), :], dst.at[k], dst_sem).wait()

    for slot in range(2):
        @pl.when(i % 2 == slot)
        def _(slot=slot):
            _combine_step(slot, i, n_steps, tc, half, gather_rows, wait_tile, buf, sem,
                          tw_ref, gate_ref, h_ref, ng_ref, nsh_ref, nsc_ref, h_new, u_ref)


def _combine_step(slot, i, n_steps, tc, half, gather_rows, wait_tile, buf, sem,
                  tw_ref, gate_ref, h_ref, ng_ref, nsh_ref, nsc_ref, h_new, u_ref):
    cur, nxt = buf[slot], buf[1 - slot]
    cur_sem, nxt_sem = sem.at[slot], sem.at[1 - slot]

    @pl.when(i == 0)
    def _():
        lax.fori_loop(0, tc, lambda r, c: (gather_rows(r * TOP_K, cur, cur_sem, r, 1), c)[1], 0)

    wait_tile(cur, cur_sem)
    nxt_pairs = jnp.minimum(i + 1, n_steps - 1) * (tc * TOP_K)

    def rows(r, carry):
        r0 = pl.multiple_of(r * SUBLANE, SUBLANE)
        gather_rows(nxt_pairs + r0 * TOP_K, nxt, nxt_sem, r0, SUBLANE)
        tw = tw_ref[0, pl.ds(r0, SUBLANE), :]
        wks = [jnp.broadcast_to(tw[:, k:k + 1], (SUBLANE, COMBINE_LANES)) for k in range(TOP_K)]
        for c in range(half // COMBINE_LANES):
            lanes = slice(c * COMBINE_LANES, (c + 1) * COMBINE_LANES)
            f_lo, f_hi = None, None
            for k in range(TOP_K):
                lo, hi = _unpack_pair(cur[k, pl.ds(r0, SUBLANE), lanes])
                f_lo = lo * wks[k] if f_lo is None else f_lo + lo * wks[k]
                f_hi = hi * wks[k] if f_hi is None else f_hi + hi * wks[k]
            hi_lanes = slice(half + c * COMBINE_LANES, half + (c + 1) * COMBINE_LANES)
            h_new[pl.ds(r0, SUBLANE), lanes] = (h_ref[0, pl.ds(r0, SUBLANE), lanes]
                                                + gate_ref[:, lanes] * f_lo)
            h_new[pl.ds(r0, SUBLANE), hi_lanes] = (h_ref[0, pl.ds(r0, SUBLANE), hi_lanes]
                                                   + gate_ref[:, hi_lanes] * f_hi)
        x = h_new[pl.ds(r0, SUBLANE), :]
        y = _rms(x) * ng_ref[...] * (1.0 + nsc_ref[...]) + nsh_ref[...]
        u_ref[0, pl.ds(r0, SUBLANE), :] = y.astype(u_ref.dtype)
        return carry

    lax.fori_loop(0, tc // SUBLANE, rows, 0, unroll=4)

    @pl.when(i == n_steps - 1)
    def _():
        wait_tile(nxt, nxt_sem)


def _combine(ys, pos, top_w, h, ml, row_of_batch, next_g, next_ml, next_dtype, emit_h):
    b, l, d = h.shape
    n_exp = top_w.shape[-1]
    tc = _tile(l, COMBINE_TILE)
    blk = pl.BlockSpec((1, tc, d), lambda bi, i, p: (bi, i, 0))
    mod = lambda chunk: _mod_spec(chunk, d, lambda bi, i, p: row_of_batch(bi), lambda bi, i, p: 0)
    u_shape = jax.ShapeDtypeStruct((b, l, d), next_dtype)
    scratch = [pltpu.VMEM((TOP_K, tc, d // 2), ys.dtype), pltpu.VMEM((TOP_K, tc, d // 2), ys.dtype),
               pltpu.SemaphoreType.DMA((2,))]
    if not emit_h:
        scratch.append(pltpu.VMEM((tc, d), F32))
    out = pl.pallas_call(
        functools.partial(_combine_kernel, emit_h=emit_h),
        grid_spec=pltpu.PrefetchScalarGridSpec(
            num_scalar_prefetch=1,
            grid=(b, l // tc),
            in_specs=[pl.BlockSpec(memory_space=pl.ANY),
                      pl.BlockSpec((1, tc, n_exp), lambda bi, i, p: (bi, i, 0)),
                      mod(5), blk,
                      pl.BlockSpec((1, d), lambda bi, i, p: (0, 0)), mod(0), mod(1)],
            out_specs=[blk, blk] if emit_h else blk,
            scratch_shapes=scratch),
        out_shape=[jax.ShapeDtypeStruct((b, l, d), F32), u_shape] if emit_h else u_shape,
        input_output_aliases={4: 0} if emit_h else {},
        compiler_params=_params(),
        name="moe_combine",
    )(pos, ys, top_w, ml, h, next_g.reshape(1, d), next_ml, next_ml)
    return out if emit_h else (None, out)


def _moe(h, g, ml, row_of_batch, layer, w_r, b_r, w_gu, b_gu, w_dn, b_dn, next_g, next_ml, next_dtype, emit_h):
    b, l, d = h.shape
    n_exp = w_r.shape[-1]
    v, top_w, top_i, rank, tile_counts = _router(h, g, ml, row_of_batch, w_r[layer], b_r[layer])
    per_tile = lambda a: a[..., :TOP_K].reshape(tile_counts.shape[0] * tile_counts.shape[1], -1, TOP_K)
    pos, tile_e, tile_valid, n_tiles, pad = _route_tables(
        per_tile(top_i), per_tile(rank), tile_counts.reshape(-1, n_exp), n_exp)
    xs = _dispatch(v.reshape(b * l, d // 2), pos, pad, n_tiles * EXPERT_TILE)
    ys = _experts(xs, tile_e, tile_valid, layer, w_gu, b_gu, w_dn, b_dn)
    return _combine(ys, pos, top_w, h, ml, row_of_batch, next_g, next_ml, next_dtype, emit_h)


def kernel(x, c, ctx, c_ctx, norm1_g, norm2_g, final_g, mod_a, mod_b, mod_bias, w_o, b_o, diff_w_in, diff_lambda, diff_subln_g, gqa_w_in, gqa_q_norm, gqa_k_norm, conv_pw1, conv_pw1_b, conv_dw, conv_dw_b, conv_ln_g, conv_ln_b, router_w, router_b, exp_w_gu, exp_b_gu, exp_w_dn, exp_b_dn):
    b, s, d = x.shape
    lc = ctx.shape[1]
    depth = mod_a.shape[0]
    rows = s // GRID_W
    cos, sin = _rope_tables(rows)
    ctx_last = max([i for i in range(depth) if i % N_MIXERS in ATTN_KINDS], default=-1)

    n_rows = -(-(b + 1) // SUBLANE) * SUBLANE
    cond = jnp.zeros((n_rows, d), F32).at[:b].set(c).at[b].set(c_ctx)
    m_all = _adaln(cond, mod_a, mod_b, mod_bias).reshape(depth, n_rows, 6, 1, d)
    lat_row = lambda bi: bi
    ctx_row = lambda bi: b

    h_lat, h_ctx = x, ctx
    u_lat = _norm_mod(h_lat, norm1_g[0], m_all[0], 0, lat_row)
    u_ctx = _norm_mod(h_ctx, norm1_g[0], m_all[0], 0, ctx_row)
    for i in range(depth):
        kind = i % N_MIXERS
        j = i // N_MIXERS
        upd_ctx = i < ctx_last
        ml = m_all[i]
        o_ctx = None
        if kind == 0:
            o_lat = _fourier_mix(u_lat)
            if upd_ctx:
                o_ctx = _fourier_mix(u_ctx)
        elif kind == 1:
            lam_init = 0.8 - 0.6 * math.exp(-0.3 * i)
            qkv_lat = _qkv(u_lat.reshape(b * s, d), diff_w_in, j, cos, sin, d, d, True).reshape(b, s, 3 * d)
            qkv_ctx = _qkv(u_ctx.reshape(b * lc, d), diff_w_in, j, cos, sin, d, d, False).reshape(b, lc, 3 * d)
            o_lat = _attn_diff(qkv_lat, [qkv_ctx, qkv_lat], diff_lambda[j], diff_subln_g[j], lam_init, d)
            if upd_ctx:
                o_ctx = _attn_diff(qkv_ctx, [qkv_ctx], diff_lambda[j], diff_subln_g[j], lam_init, d)
        elif kind == 2:
            n_heads = d // HEAD_DIM
            nq, nk = n_heads * HEAD_DIM, n_heads // GQA_GROUP * HEAD_DIM
            gains = (gqa_q_norm[j], gqa_k_norm[j])
            qkv_lat = _qkv(u_lat.reshape(b * s, d), gqa_w_in, j, cos, sin, nq, nk, True, gains).reshape(b, s, -1)
            qkv_ctx = _qkv(u_ctx.reshape(b * lc, d), gqa_w_in, j, cos, sin, nq, nk, False, gains).reshape(b, lc, -1)
            o_lat = _attn_gqa(qkv_lat, [qkv_ctx, qkv_lat], n_heads)
            if upd_ctx:
                o_ctx = _attn_gqa(qkv_ctx, [qkv_ctx], n_heads)
        else:
            def conv(u):
                bb, ll, _ = u.shape
                hg = _linear_glu(u.reshape(bb * ll, d), conv_pw1, j, conv_pw1_b[j]).reshape(bb, ll, d)
                return _conv_ln_silu(hg, conv_dw[j], conv_dw_b[j], conv_ln_g[j], conv_ln_b[j])

            o_lat = conv(u_lat)
            if upd_ctx:
                o_ctx = conv(u_ctx)

        moe_w = (i, router_w, router_b, exp_w_gu, exp_b_gu, exp_w_dn, exp_b_dn)
        last = i == depth - 1
        nxt = ((final_g, jnp.zeros_like(ml), F32, False) if last
               else (norm1_g[i + 1], m_all[i + 1], BF16, True))
        h_lat = _linear_residual(o_lat, w_o, i, b_o[i], h_lat, ml, 2, lat_row, in_place=i > 0)
        if upd_ctx:
            h_ctx = _linear_residual(o_ctx, w_o, i, b_o[i], h_ctx, ml, 2, ctx_row, in_place=i > 0)
            h_ctx, u_ctx = _moe(h_ctx, norm2_g[i], ml, ctx_row, *moe_w, *nxt)
        h_lat, u_lat = _moe(h_lat, norm2_g[i], ml, lat_row, *moe_w, *nxt)

    return u_lat
```
